```python
import math
import jax, jax.numpy as jnp
from jax import lax
import numpy as np

D_MODEL = 4096
BATCH = 2
SEQ = 4096
DEPTH = 1
DEC_BATCH = 32
DEC_SEQ = 1
PAST_LEN = 8192
PAGE_SIZE = 128

HEAD_DIM = 128
MIX_WIDTH = D_MODEL
H_DIFF = MIX_WIDTH // 2 // (2 * HEAD_DIM)
H_SB = MIX_WIDTH // 2 // HEAD_DIM
W_DIFF = H_DIFF * 2 * HEAD_DIM
W_SB = H_SB * HEAD_DIM
D_IN = 3 * W_DIFF + 3 * W_SB
D_FF = 4 * D_MODEL
ROPE_THETA = 10000.0
Q_BLOCK = 128
LN_EPS = 1e-5
SUBLN_EPS = 1e-5
DN_ALPHA = (2.0 * DEPTH) ** 0.25
DN_BETA = (8.0 * DEPTH) ** -0.25

kernel_name = "hymba_diff_stickbreak_decode_step"


def lambda_init(layer):
    return 0.8 - 0.6 * math.exp(-0.3 * layer)


def layer_norm(x, g, b):
    xf = x.astype(jnp.float32)
    mu = jnp.mean(xf, axis=-1, keepdims=True)
    var = jnp.mean(jnp.square(xf - mu), axis=-1, keepdims=True)
    return ((xf - mu) * lax.rsqrt(var + LN_EPS) * g + b).astype(x.dtype)


def rms_norm(x, g):
    xf = x.astype(jnp.float32)
    return (xf * lax.rsqrt(jnp.mean(jnp.square(xf), axis=-1, keepdims=True) + SUBLN_EPS) * g).astype(x.dtype)


def rope(x, pos):
    d = x.shape[-1]
    inv = ROPE_THETA ** (-jnp.arange(0, d, 2, dtype=jnp.float32) / d)
    ang = pos.astype(jnp.float32)[:, None] * inv[None, :]
    shape = (1, ang.shape[0]) + (1,) * (x.ndim - 3) + (d // 2,)
    cos = jnp.cos(ang).reshape(shape)
    sin = jnp.sin(ang).reshape(shape)
    xf = x.astype(jnp.float32)
    x1, x2 = xf[..., : d // 2], xf[..., d // 2:]
    return jnp.concatenate([x1 * cos - x2 * sin, x2 * cos + x1 * sin], axis=-1).astype(x.dtype)


def project(x, pos, w_in):
    b, t, _ = x.shape
    u = x @ w_in
    splits = [W_DIFF, 2 * W_DIFF, 3 * W_DIFF, 3 * W_DIFF + W_SB, 3 * W_DIFF + 2 * W_SB]
    dq, dk, dv, sq, sk, sv = jnp.split(u, splits, axis=-1)
    dq = rope(dq.reshape(b, t, H_DIFF, 2, HEAD_DIM), pos).reshape(b, t, H_DIFF, 2 * HEAD_DIM)
    dk = rope(dk.reshape(b, t, H_DIFF, 2, HEAD_DIM), pos).reshape(b, t, H_DIFF, 2 * HEAD_DIM)
    dv = dv.reshape(b, t, H_DIFF, 2 * HEAD_DIM)
    sq = sq.reshape(b, t, H_SB, HEAD_DIM)
    sk = sk.reshape(b, t, H_SB, HEAD_DIM)
    sv = sv.reshape(b, t, H_SB, HEAD_DIM)
    return dq, dk, dv, sq, sk, sv


def diff_attention(q, k, v, q_pos, k_pos, lam):
    scale = HEAD_DIM ** -0.5
    mask = k_pos[None, :] <= q_pos[:, None]

    def probs(qi, ki):
        s = jnp.einsum('bqhd,bkhd->bhqk', qi, ki, preferred_element_type=jnp.float32) * scale
        return jax.nn.softmax(jnp.where(mask, s, -jnp.inf), axis=-1)

    a = probs(q[..., :HEAD_DIM], k[..., :HEAD_DIM]) - lam * probs(q[..., HEAD_DIM:], k[..., HEAD_DIM:])
    return jnp.einsum('bhqk,bkhd->bqhd', a.astype(v.dtype), v)


def stick_breaking(q, k, v, q_pos, k_pos):
    scale = HEAD_DIM ** -0.5
    z = jnp.einsum('bqhd,bkhd->bhqk', q, k, preferred_element_type=jnp.float32) * scale
    mask = k_pos[None, :] < q_pos[:, None]
    log_1m = jnp.where(mask, jax.nn.log_sigmoid(-z), 0.0)
    key_axis = log_1m.ndim - 1
    log_surv = lax.cumsum(log_1m, axis=key_axis, reverse=True) - log_1m
    w = jnp.where(mask, jnp.exp(jax.nn.log_sigmoid(z) + log_surv), 0.0)
    return jnp.einsum('bhqk,bkhd->bqhd', w.astype(v.dtype), v)


def finish_layer(x, o_d, o_s, layer, subln_g, w_out, ln1_g, ln1_b, w_up, w_down, ln2_g, ln2_b):
    b, t, _ = x.shape
    o_d = rms_norm(o_d, subln_g) * (1.0 - lambda_init(layer))
    mix = jnp.concatenate([o_d.reshape(b, t, W_DIFF), o_s.reshape(b, t, W_SB)], axis=-1)
    h = layer_norm(DN_ALPHA * x + mix @ w_out, ln1_g, ln1_b)
    f = jnp.square(jax.nn.relu(h @ w_up)) @ w_down
    return layer_norm(DN_ALPHA * h + f, ln2_g, ln2_b)


def setup_inputs(seed: int = 0) -> dict:
    key = jax.random.key(seed)
    ks = jax.random.split(key, 24)
    n_pages = PAST_LEN // PAGE_SIZE
    n_pool = (5 * DEC_BATCH * n_pages) // 4
    nrm = lambda k, s: jax.random.normal(k, s, dtype=jnp.float32)
    x_prompt = nrm(ks[0], (BATCH, SEQ, D_MODEL))
    x_sample = nrm(ks[1], (DEC_BATCH, DEC_SEQ, D_MODEL))
    cache_diff_k = nrm(ks[2], (DEPTH, n_pool, PAGE_SIZE, H_DIFF, 2 * HEAD_DIM))
    cache_diff_v = nrm(ks[3], (DEPTH, n_pool, PAGE_SIZE, H_DIFF, 2 * HEAD_DIM)) * DN_BETA
    cache_sb_k = nrm(ks[4], (DEPTH, n_pool, PAGE_SIZE, H_SB, HEAD_DIM))
    cache_sb_v = nrm(ks[5], (DEPTH, n_pool, PAGE_SIZE, H_SB, HEAD_DIM)) * DN_BETA
    page_table = jax.random.permutation(ks[6], n_pool)[: DEC_BATCH * n_pages].reshape(DEC_BATCH, n_pages).astype(jnp.int32)
    col_scale = jnp.concatenate([
        jnp.ones((2 * W_DIFF,), jnp.float32), jnp.full((W_DIFF,), DN_BETA, jnp.float32),
        jnp.ones((2 * W_SB,), jnp.float32), jnp.full((W_SB,), DN_BETA, jnp.float32)])
    w_in = nrm(ks[7], (DEPTH, D_MODEL, D_IN)) * (D_MODEL ** -0.5) * col_scale
    lambda_q1 = 0.1 * nrm(ks[8], (DEPTH, HEAD_DIM))
    lambda_k1 = 0.1 * nrm(ks[9], (DEPTH, HEAD_DIM))
    lambda_q2 = 0.1 * nrm(ks[10], (DEPTH, HEAD_DIM))
    lambda_k2 = 0.1 * nrm(ks[11], (DEPTH, HEAD_DIM))
    subln_g = 1.0 + 0.02 * nrm(ks[12], (DEPTH, 2 * HEAD_DIM))
    w_out = nrm(ks[13], (DEPTH, MIX_WIDTH, D_MODEL)) * (MIX_WIDTH ** -0.5) * DN_BETA
    ln1_g = 1.0 + 0.02 * nrm(ks[14], (DEPTH, D_MODEL))
    ln1_b = 0.02 * nrm(ks[15], (DEPTH, D_MODEL))
    w_up = nrm(ks[16], (DEPTH, D_MODEL, D_FF)) * (D_MODEL ** -0.5)
    w_down = nrm(ks[17], (DEPTH, D_FF, D_MODEL)) * (D_FF ** -0.5) * DN_BETA
    ln2_g = 1.0 + 0.02 * nrm(ks[18], (DEPTH, D_MODEL))
    ln2_b = 0.02 * nrm(ks[19], (DEPTH, D_MODEL))
    return {"x_prompt": x_prompt, "x_sample": x_sample,
            "cache_diff_k": cache_diff_k, "cache_diff_v": cache_diff_v,
            "cache_sb_k": cache_sb_k, "cache_sb_v": cache_sb_v,
            "page_table": page_table, "w_in": w_in,
            "lambda_q1": lambda_q1, "lambda_k1": lambda_k1, "lambda_q2": lambda_q2, "lambda_k2": lambda_k2,
            "subln_g": subln_g, "w_out": w_out, "ln1_g": ln1_g, "ln1_b": ln1_b,
            "w_up": w_up, "w_down": w_down, "ln2_g": ln2_g, "ln2_b": ln2_b}


def reference(x_prompt, x_sample, cache_diff_k, cache_diff_v, cache_sb_k, cache_sb_v, page_table,
              w_in, lambda_q1, lambda_k1, lambda_q2, lambda_k2, subln_g, w_out,
              ln1_g, ln1_b, w_up, w_down, ln2_g, ln2_b):
    b_p, seq = x_prompt.shape[0], x_prompt.shape[1]
    dec_seq = x_sample.shape[1]
    n_pages = PAST_LEN // PAGE_SIZE
    n_blocks = seq // Q_BLOCK
    pos_p = jnp.arange(seq, dtype=jnp.int32)
    pos_s = PAST_LEN + jnp.arange(dec_seq, dtype=jnp.int32)
    pos_all = jnp.arange(PAST_LEN + dec_seq, dtype=jnp.int32)

    xp, xs = x_prompt, x_sample
    pdk, pdv, psk, psv = [], [], [], []
    sdk, sdv, ssk, ssv = [], [], [], []
    for l in range(DEPTH):
        lam = (jnp.exp(jnp.sum(lambda_q1[l].astype(jnp.float32) * lambda_k1[l].astype(jnp.float32)))
               - jnp.exp(jnp.sum(lambda_q2[l].astype(jnp.float32) * lambda_k2[l].astype(jnp.float32)))
               + lambda_init(l))
        wl = (subln_g[l], w_out[l], ln1_g[l], ln1_b[l], w_up[l], w_down[l], ln2_g[l], ln2_b[l])

        dq, dk, dv, sq, sk, sv = project(xp, pos_p, w_in[l])

        def to_blocks(q):
            return q.reshape((b_p, n_blocks, Q_BLOCK) + q.shape[2:]).swapaxes(0, 1)

        def prompt_block(args):
            qd, qs, qp = args
            return (diff_attention(qd, dk, dv, qp, pos_p, lam),
                    stick_breaking(qs, sk, sv, qp, pos_p))

        od, osb = lax.map(prompt_block, (to_blocks(dq), to_blocks(sq), pos_p.reshape(n_blocks, Q_BLOCK)))
        od = od.swapaxes(0, 1).reshape(b_p, seq, H_DIFF, 2 * HEAD_DIM)
        osb = osb.swapaxes(0, 1).reshape(b_p, seq, H_SB, HEAD_DIM)
        pdk.append(dk); pdv.append(dv); psk.append(sk); psv.append(sv)
        xp_next = finish_layer(xp, od, osb, l, *wl)

        qd_s, kd_s, vd_s, qs_s, ks_s, vs_s = project(xs, pos_s, w_in[l])
        ck_d, cv_d, ck_s, cv_s = cache_diff_k[l], cache_diff_v[l], cache_sb_k[l], cache_sb_v[l]

        def sample_seq(args):
            pt, qd, kd, vd, qs, kss, vss = args

            def gather(cache, new):
                past = cache[pt].reshape((n_pages * PAGE_SIZE,) + cache.shape[2:])
                return jnp.concatenate([past, new], axis=0)[None]

            o1 = diff_attention(qd[None], gather(ck_d, kd), gather(cv_d, vd), pos_s, pos_all, lam)[0]
            o2 = stick_breaking(qs[None], gather(ck_s, kss), gather(cv_s, vss), pos_s, pos_all)[0]
            return o1, o2

        od_s, osb_s = lax.map(sample_seq, (page_table, qd_s, kd_s, vd_s, qs_s, ks_s, vs_s))
        sdk.append(kd_s); sdv.append(vd_s); ssk.append(ks_s); ssv.append(vs_s)
        xs = finish_layer(xs, od_s, osb_s, l, *wl)
        xp = xp_next

    return (xp, xs,
            jnp.stack(pdk), jnp.stack(pdv), jnp.stack(psk), jnp.stack(psv),
            jnp.stack(sdk), jnp.stack(sdv), jnp.stack(ssk), jnp.stack(ssv))
```

```python
import functools
import math

import jax
import jax.numpy as jnp
from jax import lax
from jax.experimental import pallas as pl
from jax.experimental.pallas import tpu as pltpu

HEAD_DIM = 128
ROPE_THETA = 10000.0
LN_EPS = 1e-5
SUBLN_EPS = 1e-5
EXP_UNDERFLOW = -104.0

V7X_LANES = 128
V7X_VMEM_BYTES = 64 * 1024 * 1024
V7X_VMEM_REQUEST_CAP = V7X_VMEM_BYTES - 4 * 1024 * 1024

_F32 = jnp.float32
_BF16 = jnp.bfloat16
_NT = (((1,), (1,)), ((), ()))


def _lambda_init(layer):
    return 0.8 - 0.6 * math.exp(-0.3 * layer)


def _tile(dim, pref):
    t = min(dim, pref)
    assert dim % t == 0, (dim, pref)
    return t


def _params(semantics, vmem_estimate_bytes):
    limit = min(max(int(vmem_estimate_bytes * 1.2), 32 * 1024 * 1024), V7X_VMEM_REQUEST_CAP)
    return pltpu.CompilerParams(dimension_semantics=semantics, vmem_limit_bytes=limit)


def _proj_kernel(x_ref, w_ref, cos_ref, sin_ref, *out_refs, rope, scale):
    u = jnp.dot(x_ref[...], w_ref[...], preferred_element_type=_F32)
    tn = u.shape[1]
    for c in range(tn // HEAD_DIM):
        sl = slice(c * HEAD_DIM, (c + 1) * HEAD_DIM)
        uc = u[:, sl]
        if rope:
            uc = uc * cos_ref[...] + pltpu.roll(uc, HEAD_DIM // 2, 1) * sin_ref[...]
        if scale != 1.0:
            uc = uc * scale
        for o_ref in out_refs:
            o_ref[:, sl] = uc.astype(o_ref.dtype)


def _project(x_b, w_b, cos, sin, col_off, n_cols, *, rope, scale, out_dtypes, tm_pref):
    m, d = x_b.shape
    tm = _tile(cos.shape[0], tm_pref)
    tn = _tile(n_cols, 512)
    assert col_off % tn == 0 and m % tm == 0
    pos_blocks = cos.shape[0] // tm
    est = 2 * (tm * d * 2 + d * tn * 2 + 2 * tm * HEAD_DIM * 4) + tm * tn * 4
    est += sum(2 * tm * tn * jnp.dtype(dt).itemsize for dt in out_dtypes)
    return pl.pallas_call(
        functools.partial(_proj_kernel, rope=rope, scale=scale),
        grid=(m // tm, n_cols // tn),
        in_specs=[
            pl.BlockSpec((tm, d), lambda i, j: (i, 0)),
            pl.BlockSpec((d, tn), lambda i, j: (0, col_off // tn + j)),
            pl.BlockSpec((tm, HEAD_DIM), lambda i, j: (i % pos_blocks, 0)),
            pl.BlockSpec((tm, HEAD_DIM), lambda i, j: (i % pos_blocks, 0)),
        ],
        out_specs=[pl.BlockSpec((tm, tn), lambda i, j: (i, j)) for _ in out_dtypes],
        out_shape=[jax.ShapeDtypeStruct((m, n_cols), dt) for dt in out_dtypes],
        compiler_params=_params(("parallel", "arbitrary"), est),
        name="proj",
    )(x_b, w_b, cos, sin)


def _lam_from_params(lam_ref, lam_init):
    p = lam_ref[...]
    a = jnp.sum(p[0:1] * p[1:2], axis=1, keepdims=True)
    b = jnp.sum(p[2:3] * p[3:4], axis=1, keepdims=True)
    return jnp.exp(a) - jnp.exp(b) + lam_init


def _diff_attn_kernel(lam_ref, g_ref, q_ref, k_ref, v_ref, o_ref, acc1_ref, acc2_ref, *, t, lam_init):
    i = pl.program_id(2)
    q = q_ref[0]
    qs = (q[:, :HEAD_DIM], q[:, HEAD_DIM:])
    accs = (acc1_ref, acc2_ref)
    acc1_ref[...] = jnp.zeros_like(acc1_ref)
    acc2_ref[...] = jnp.zeros_like(acc2_ref)

    def block(kb, carry, masked):
        start = pl.multiple_of(kb * t, t)
        k = k_ref[0, pl.ds(start, t), :]
        v = v_ref[0, pl.ds(start, t), :]
        new = []
        for c in range(2):
            m, l = carry[2 * c], carry[2 * c + 1]
            s = lax.dot_general(qs[c], k[:, c * HEAD_DIM:(c + 1) * HEAD_DIM], _NT,
                                preferred_element_type=_F32)
            if masked:
                row = lax.broadcasted_iota(jnp.int32, s.shape, 0)
                col = lax.broadcasted_iota(jnp.int32, s.shape, 1)
                s = jnp.where(col <= row, s, -jnp.inf)
            m_new = jnp.maximum(m, jnp.max(s, axis=1, keepdims=True))
            p = jnp.exp(s - m_new)
            alpha = jnp.exp(m - m_new)
            l_new = alpha * l + jnp.sum(p, axis=1, keepdims=True)
            accs[c][...] = alpha * accs[c][...] + jnp.dot(p.astype(_BF16), v, preferred_element_type=_F32)
            new += [m_new, l_new]
        return tuple(new)

    neg = jnp.full((t, 1), -jnp.inf, _F32)
    zero = jnp.zeros((t, 1), _F32)
    carry = lax.fori_loop(0, i, lambda kb, c: block(kb, c, False), (neg, zero, neg, zero))
    _, l1, _, l2 = block(i, carry, True)

    lam = _lam_from_params(lam_ref, lam_init)
    o = acc1_ref[...] / l1 - lam * (acc2_ref[...] / l2)
    ms = jnp.mean(o * o, axis=1, keepdims=True)
    o_ref[0] = (o * lax.rsqrt(ms + SUBLN_EPS) * g_ref[...] * (1.0 - lam_init)).astype(o_ref.dtype)


def _diff_attention(lam_p, g, q, k, v, *, lam_init):
    b, s, w = q.shape
    hw = 2 * HEAD_DIM
    t = _tile(s, 256)
    est = 2 * (2 * t * hw * 2 + 2 * s * hw * 2) + 2 * t * hw * 4 + 8 * t * t * 4
    return pl.pallas_call(
        functools.partial(_diff_attn_kernel, t=t, lam_init=lam_init),
        grid=(b, w // hw, s // t),
        in_specs=[
            pl.BlockSpec((4, HEAD_DIM), lambda bi, h, i: (0, 0)),
            pl.BlockSpec((1, hw), lambda bi, h, i: (0, 0)),
            pl.BlockSpec((1, t, hw), lambda bi, h, i: (bi, i, h)),
            pl.BlockSpec((1, s, hw), lambda bi, h, i: (bi, 0, h)),
            pl.BlockSpec((1, s, hw), lambda bi, h, i: (bi, 0, h)),
        ],
        out_specs=pl.BlockSpec((1, t, hw), lambda bi, h, i: (bi, i, h)),
        out_shape=jax.ShapeDtypeStruct((b, s, w), _BF16),
        scratch_shapes=[pltpu.VMEM((t, hw), _F32), pltpu.VMEM((t, hw), _F32)],
        compiler_params=_params(("parallel", "parallel", "arbitrary"), est),
        name="diff_attn",
    )(lam_p, g, q, k, v)


def _suffix_sum_matrix(n):
    j = lax.broadcasted_iota(jnp.int32, (n, n), 0)
    s = lax.broadcasted_iota(jnp.int32, (n, n), 1)
    return (j >= s).astype(_BF16)


def _log_one_minus_beta(z):
    return -(jnp.maximum(z, 0.0) + jnp.log(1.0 + jnp.exp(-jnp.abs(z))))


def _suffix_sums(log_1m, tmat):
    hi = log_1m.astype(_BF16)
    lo = (log_1m - hi.astype(_F32)).astype(_BF16)
    return (jnp.dot(hi, tmat, preferred_element_type=_F32)
            + jnp.dot(lo, tmat, preferred_element_type=_F32))


def _sb_attn_kernel(q_ref, k_ref, v_ref, o_ref, acc_ref, *, t):
    i = pl.program_id(2)
    q = q_ref[0]
    tmat = _suffix_sum_matrix(t)

    def block(kb, carry, masked):
        start = pl.multiple_of(kb * t, t)
        k = k_ref[0, pl.ds(start, t), :]
        v = v_ref[0, pl.ds(start, t), :]
        z = lax.dot_general(q, k, _NT, preferred_element_type=_F32)
        log_1m = _log_one_minus_beta(z)
        log_b = log_1m + z
        if masked:
            row = lax.broadcasted_iota(jnp.int32, z.shape, 0)
            col = lax.broadcasted_iota(jnp.int32, z.shape, 1)
            valid = col < row
            log_1m = jnp.where(valid, log_1m, 0.0)
        incl = _suffix_sums(log_1m, tmat)
        w = jnp.exp(log_b + (incl - log_1m) + carry)
        if masked:
            w = jnp.where(valid, w, 0.0)
        acc_ref[...] += jnp.dot(w.astype(_BF16), v, preferred_element_type=_F32)
        return carry + jnp.sum(log_1m, axis=1, keepdims=True)

    acc_ref[...] = jnp.zeros_like(acc_ref)
    carry = block(i, jnp.zeros((t, 1), _F32), True)

    def cond(state):
        kb, _, live = state
        return jnp.logical_and(kb >= 0, live)

    def body(state):
        kb, carry, _ = state
        carry = block(kb, carry, False)
        return kb - 1, carry, jnp.max(carry) > EXP_UNDERFLOW

    lax.while_loop(cond, body, (i - 1, carry, jnp.max(carry) > EXP_UNDERFLOW))
    o_ref[0] = acc_ref[...].astype(o_ref.dtype)


def _sb_attention(q, k, v):
    b, s, w = q.shape
    t = _tile(s, 256)
    est = 2 * (2 * t * HEAD_DIM * 2 + 2 * s * HEAD_DIM * 2) + t * HEAD_DIM * 4 + 10 * t * t * 4
    return pl.pallas_call(
        functools.partial(_sb_attn_kernel, t=t),
        grid=(b, w // HEAD_DIM, s // t),
        in_specs=[
            pl.BlockSpec((1, t, HEAD_DIM), lambda bi, h, i: (bi, i, h)),
            pl.BlockSpec((1, s, HEAD_DIM), lambda bi, h, i: (bi, 0, h)),
            pl.BlockSpec((1, s, HEAD_DIM), lambda bi, h, i: (bi, 0, h)),
        ],
        out_specs=pl.BlockSpec((1, t, HEAD_DIM), lambda bi, h, i: (bi, i, h)),
        out_shape=jax.ShapeDtypeStruct((b, s, w), _BF16),
        scratch_shapes=[pltpu.VMEM((t, HEAD_DIM), _F32)],
        compiler_params=_params(("parallel", "parallel", "arbitrary"), est),
        name="sb_attn",
    )(q, k, v)


PAGES_PER_STEP = 4


def _own_head_mask(n_rows, n_cols, n_heads):
    row = lax.broadcasted_iota(jnp.int32, (n_rows, n_cols), 0)
    col = lax.broadcasted_iota(jnp.int32, (n_rows, n_cols), 1)
    return col % n_heads == row % n_heads


def _sample_diff_kernel(pt_ref, lam_ref, g_ref, q_ref, knew_ref, vnew_ref, *refs, pp, lam_init):
    k_pages, v_pages = refs[:pp], refs[pp:2 * pp]
    o_ref, qm_ref, m_ref, l_ref, acc_ref = refs[2 * pp:]
    del pt_ref
    p = pl.program_id(1)
    page, nh, hw = k_pages[0].shape
    twice = lambda a: jnp.concatenate([a, a], axis=0)

    @pl.when(p == 0)
    def _():
        row = lax.broadcasted_iota(jnp.int32, (2 * nh, hw), 0)
        lane = lax.broadcasted_iota(jnp.int32, (2 * nh, hw), 1)
        qm_ref[...] = jnp.where(lane // HEAD_DIM == row // nh, twice(q_ref[...]), 0.0)
        m_ref[...] = jnp.full_like(m_ref, -jnp.inf)
        l_ref[...] = jnp.zeros_like(l_ref)
        acc_ref[...] = jnp.zeros_like(acc_ref)

    def update(s, pv_fn):
        m = m_ref[...]
        m_new = jnp.maximum(m, jnp.max(s, axis=1, keepdims=True))
        pr = jnp.exp(s - m_new)
        alpha = jnp.exp(m - m_new)
        l_ref[...] = alpha * l_ref[...] + jnp.sum(pr, axis=1, keepdims=True)
        acc_ref[...] = alpha * acc_ref[...] + pv_fn(pr)
        m_ref[...] = m_new

    qm = qm_ref[...].astype(_BF16)
    own = _own_head_mask(2 * nh, page * nh, nh)
    for slot in range(pp):
        kp = k_pages[slot][...].reshape(page * nh, hw).astype(_BF16)
        vp = v_pages[slot][...].reshape(page * nh, hw).astype(_BF16)
        s = lax.dot_general(qm, kp, _NT, preferred_element_type=_F32)
        s = jnp.where(own, s, -jnp.inf)
        update(s, lambda pr, vp=vp: jnp.dot(pr.astype(_BF16), vp, preferred_element_type=_F32))

    @pl.when(p == pl.num_programs(1) - 1)
    def _():
        s_new = jnp.sum(qm_ref[...] * twice(knew_ref[...]), axis=1, keepdims=True)
        update(s_new, lambda pr: pr * twice(vnew_ref[...]))
        lam = _lam_from_params(lam_ref, lam_init)
        norm = acc_ref[...] / l_ref[...]
        o = norm[:nh] - lam * norm[nh:]
        ms = jnp.mean(o * o, axis=1, keepdims=True)
        o_ref[...] = o * lax.rsqrt(ms + SUBLN_EPS) * g_ref[...] * (1.0 - lam_init)


def _sample_sb_kernel(pt_ref, q_ref, *refs, pp):
    k_pages, v_pages = refs[:pp], refs[pp:2 * pp]
    o_ref, carry_ref, acc_ref = refs[2 * pp:]
    del pt_ref
    p = pl.program_id(1)
    page, nh, d = k_pages[0].shape
    n_cols = page * nh

    @pl.when(p == 0)
    def _():
        carry_ref[...] = jnp.zeros_like(carry_ref)
        acc_ref[...] = jnp.zeros_like(acc_ref)

    for slot in range(pp):
        @pl.when(jnp.max(carry_ref[...]) > EXP_UNDERFLOW)
        def _(slot=slot):
            kp = k_pages[slot][...].reshape(n_cols, d).astype(_BF16)
            vp = v_pages[slot][...].reshape(n_cols, d).astype(_BF16)
            z = lax.dot_general(q_ref[...].astype(_BF16), kp, _NT, preferred_element_type=_F32)
            own = _own_head_mask(nh, n_cols, nh)
            log_1m = jnp.where(own, _log_one_minus_beta(z), 0.0)
            tmat = _suffix_sum_matrix(V7X_LANES)
            running = carry_ref[...]
            w_tiles = []
            for c in reversed(range(n_cols // V7X_LANES)):
                sl = slice(c * V7X_LANES, (c + 1) * V7X_LANES)
                incl = _suffix_sums(log_1m[:, sl], tmat)
                w_tiles.append(jnp.where(own[:, sl], jnp.exp(z[:, sl] + incl + running), 0.0))
                running = running + jnp.sum(log_1m[:, sl], axis=1, keepdims=True)
            w = jnp.concatenate(w_tiles[::-1], axis=1)
            acc_ref[...] += jnp.dot(w.astype(_BF16), vp, preferred_element_type=_F32)
            carry_ref[...] = running

    @pl.when(p == pl.num_programs(1) - 1)
    def _():
        o_ref[...] = acc_ref[...]


def _page_specs(page_shape, layer, n_pages, pp, newest_first):
    def spec(slot):
        def index(b, p, pt):
            j = p * pp + slot
            if newest_first:
                j = n_pages - 1 - j
            return (layer, pt[b, j], 0, 0, 0)
        return pl.BlockSpec((None, None) + tuple(page_shape), index)
    return [spec(slot) for slot in range(pp)]


def _sample_diff_attention(page_table, lam_p, g, q, k_new, v_new, cache_k, cache_v, layer, *, lam_init):
    n, nh, hw = q.shape
    n_pages = page_table.shape[1]
    page_shape = cache_k.shape[2:]
    pp = _tile(n_pages, PAGES_PER_STEP)
    head_spec = pl.BlockSpec((None, nh, hw), lambda b, p, pt: (b, 0, 0))
    pages = _page_specs(page_shape, layer, n_pages, pp, newest_first=False)
    page_elems = math.prod(page_shape)
    est = 2 * 2 * pp * page_elems * 4 + 4 * page_elems * 2 + 4 * 2 * nh * page_shape[0] * nh * 4
    return pl.pallas_call(
        functools.partial(_sample_diff_kernel, pp=pp, lam_init=lam_init),
        grid_spec=pltpu.PrefetchScalarGridSpec(
            num_scalar_prefetch=1,
            grid=(n, n_pages // pp),
            in_specs=[
                pl.BlockSpec((4, HEAD_DIM), lambda b, p, pt: (0, 0)),
                pl.BlockSpec((1, hw), lambda b, p, pt: (0, 0)),
                head_spec, head_spec, head_spec,
            ] + pages + pages,
            out_specs=head_spec,
            scratch_shapes=[pltpu.VMEM((2 * nh, hw), _F32), pltpu.VMEM((2 * nh, 1), _F32),
                            pltpu.VMEM((2 * nh, 1), _F32), pltpu.VMEM((2 * nh, hw), _F32)],
        ),
        out_shape=jax.ShapeDtypeStruct((n, nh, hw), _F32),
        compiler_params=_params(("parallel", "arbitrary"), est),
        name="sample_diff_attn",
    )(page_table, lam_p, g, q, k_new, v_new, *([cache_k] * pp), *([cache_v] * pp))


def _sample_sb_attention(page_table, q, cache_k, cache_v, layer):
    n, nh, d = q.shape
    n_pages = page_table.shape[1]
    page_shape = cache_k.shape[2:]
    pp = _tile(n_pages, PAGES_PER_STEP)
    head_spec = pl.BlockSpec((None, nh, d), lambda b, p, pt: (b, 0, 0))
    pages = _page_specs(page_shape, layer, n_pages, pp, newest_first=True)
    page_elems = math.prod(page_shape)
    est = 2 * 2 * pp * page_elems * 4 + 4 * page_elems * 2 + 8 * nh * page_shape[0] * nh * 4
    return pl.pallas_call(
        functools.partial(_sample_sb_kernel, pp=pp),
        grid_spec=pltpu.PrefetchScalarGridSpec(
            num_scalar_prefetch=1,
            grid=(n, n_pages // pp),
            in_specs=[head_spec] + pages + pages,
            out_specs=head_spec,
            scratch_shapes=[pltpu.VMEM((nh, 1), _F32), pltpu.VMEM((nh, d), _F32)],
        ),
        out_shape=jax.ShapeDtypeStruct((n, nh, d), _F32),
        compiler_params=_params(("parallel", "arbitrary"), est),
        name="sample_sb_attn",
    )(page_table, q, *([cache_k] * pp), *([cache_v] * pp))


def _layer_norm_rows(z, g, b):
    mu = jnp.mean(z, axis=1, keepdims=True)
    zc = z - mu
    var = jnp.mean(zc * zc, axis=1, keepdims=True)
    return zc * lax.rsqrt(var + LN_EPS) * g + b


def _out_proj_kernel(od_ref, os_ref, x_ref, w_ref, g_ref, b_ref, hf_ref, hb_ref, *, alpha, tn):
    n = pl.program_id(1)
    wd = od_ref.shape[1]
    z = alpha * x_ref[...]
    z += jnp.dot(od_ref[...], w_ref[:wd, :], preferred_element_type=_F32)
    z += jnp.dot(os_ref[...], w_ref[wd:, :], preferred_element_type=_F32)
    n_tiles = hf_ref.shape[1] // tn
    for nn in range(n_tiles):
        @pl.when(n == nn)
        def _(nn=nn):
            hf_ref[:, nn * tn:(nn + 1) * tn] = z

    @pl.when(n == n_tiles - 1)
    def _():
        h = _layer_norm_rows(hf_ref[...], g_ref[...], b_ref[...])
        hf_ref[...] = h
        hb_ref[...] = h.astype(hb_ref.dtype)


def _out_proj(o_d, o_s, x, w_b, g, b, *, alpha, tm_pref):
    m, d = x.shape
    wd, ws = o_d.shape[1], o_s.shape[1]
    tm = _tile(m, tm_pref)
    tn = _tile(d, 512)
    est = 2 * (tm * (wd + ws) * 2 + tm * tn * 4 + (wd + ws) * tn * 2 + tm * d * 6) + 2 * tm * d * 4
    return pl.pallas_call(
        functools.partial(_out_proj_kernel, alpha=alpha, tn=tn),
        grid=(m // tm, d // tn),
        in_specs=[
            pl.BlockSpec((tm, wd), lambda i, n: (i, 0)),
            pl.BlockSpec((tm, ws), lambda i, n: (i, 0)),
            pl.BlockSpec((tm, tn), lambda i, n: (i, n)),
            pl.BlockSpec((wd + ws, tn), lambda i, n: (0, n)),
            pl.BlockSpec((1, d), lambda i, n: (0, 0)),
            pl.BlockSpec((1, d), lambda i, n: (0, 0)),
        ],
        out_specs=[pl.BlockSpec((tm, d), lambda i, n: (i, 0)), pl.BlockSpec((tm, d), lambda i, n: (i, 0))],
        out_shape=[jax.ShapeDtypeStruct((m, d), _F32), jax.ShapeDtypeStruct((m, d), _BF16)],
        compiler_params=_params(("parallel", "arbitrary"), est),
        name="out_proj_ln",
    )(o_d, o_s, x, w_b, g, b)


def _mlp_kernel(hb_ref, hf_ref, wu_ref, wd_ref, g_ref, b_ref, y_ref, *, alpha):
    j = pl.program_id(1)
    a = jnp.maximum(jnp.dot(hb_ref[...], wu_ref[...], preferred_element_type=_F32), 0.0)
    f = jnp.dot((a * a).astype(_BF16), wd_ref[...], preferred_element_type=_F32)

    @pl.when(j == 0)
    def _():
        y_ref[...] = alpha * hf_ref[...] + f

    @pl.when(j > 0)
    def _():
        y_ref[...] += f

    @pl.when(j == pl.num_programs(1) - 1)
    def _():
        y_ref[...] = _layer_norm_rows(y_ref[...], g_ref[...], b_ref[...])


def _mlp(h_b, h_f, wu_b, wd_b, g, b, *, alpha, tm_pref):
    m, d = h_f.shape
    dff = wu_b.shape[1]
    tm = _tile(m, tm_pref)
    tf = _tile(dff, 512)
    est = 2 * tm * d * 2 + tm * d * 4 + 2 * 2 * d * tf * 2 + tm * d * 4 + tm * tf * 6 + tm * d * 4
    single = pl.Buffered(1)
    return pl.pallas_call(
        functools.partial(_mlp_kernel, alpha=alpha),
        grid=(m // tm, dff // tf),
        in_specs=[
            pl.BlockSpec((tm, d), lambda i, j: (i, 0)),
            pl.BlockSpec((tm, d), lambda i, j: (i, 0), pipeline_mode=single),
            pl.BlockSpec((d, tf), lambda i, j: (0, j)),
            pl.BlockSpec((tf, d), lambda i, j: (j, 0)),
            pl.BlockSpec((1, d), lambda i, j: (0, 0)),
            pl.BlockSpec((1, d), lambda i, j: (0, 0)),
        ],
        out_specs=pl.BlockSpec((tm, d), lambda i, j: (i, 0), pipeline_mode=single),
        out_shape=jax.ShapeDtypeStruct((m, d), _F32),
        compiler_params=_params(("parallel", "arbitrary"), est),
        name="mlp_ln",
    )(h_b, h_f, wu_b, wd_b, g, b)


def _rope_tables(pos):
    inv = ROPE_THETA ** (-jnp.arange(0, HEAD_DIM, 2, dtype=_F32) / HEAD_DIM)
    ang = pos.astype(_F32)[:, None] * inv[None, :]
    cos, sin = jnp.cos(ang), jnp.sin(ang)
    return jnp.concatenate([cos, cos], axis=1), jnp.concatenate([-sin, sin], axis=1)


def kernel(x_prompt, x_sample, cache_diff_k, cache_diff_v, cache_sb_k, cache_sb_v, page_table, w_in, lambda_q1, lambda_k1, lambda_q2, lambda_k2, subln_g, w_out, ln1_g, ln1_b, w_up, w_down, ln2_g, ln2_b):
    bsz, seq, d = x_prompt.shape
    n_dec, dec_seq, _ = x_sample.shape
    assert dec_seq == 1
    depth, n_pool, page, h_diff, hw = cache_diff_k.shape
    h_sb = cache_sb_k.shape[3]
    assert hw == 2 * HEAD_DIM and cache_sb_k.shape[4] == HEAD_DIM
    w_diff, w_sb = h_diff * hw, h_sb * HEAD_DIM
    past_len = page_table.shape[1] * page
    alpha = (2.0 * depth) ** 0.25
    scale = HEAD_DIM ** -0.5

    cos_p, sin_p = _rope_tables(jnp.arange(seq, dtype=jnp.int32))
    cos_s, sin_s = _rope_tables(jnp.full((n_dec,), past_len, dtype=jnp.int32))

    sections = {
        "dq": (0, w_diff, True, scale), "dk": (w_diff, w_diff, True, 1.0), "dv": (2 * w_diff, w_diff, False, 1.0),
        "sq": (3 * w_diff, w_sb, False, scale), "sk": (3 * w_diff + w_sb, w_sb, False, 1.0),
        "sv": (3 * w_diff + 2 * w_sb, w_sb, False, 1.0),
    }

    xp = x_prompt.reshape(bsz * seq, d)
    xs = x_sample.reshape(n_dec, d)
    outs = {k: [] for k in ("pdk", "pdv", "psk", "psv", "sdk", "sdv", "ssk", "ssv")}
    for l in range(depth):
        lam_init = _lambda_init(l)
        lam_p = jnp.stack([lambda_q1[l], lambda_k1[l], lambda_q2[l], lambda_k2[l]]).astype(_F32)
        g_sub = subln_g[l].reshape(1, hw).astype(_F32)
        w_in_b, w_out_b = w_in[l].astype(_BF16), w_out[l].astype(_BF16)
        w_up_b, w_down_b = w_up[l].astype(_BF16), w_down[l].astype(_BF16)
        ln1 = (ln1_g[l].reshape(1, d), ln1_b[l].reshape(1, d))
        ln2 = (ln2_g[l].reshape(1, d), ln2_b[l].reshape(1, d))

        xp_b = xp.astype(_BF16)
        pr = {}
        for name, (off, width, rope, sc) in sections.items():
            dts = (_BF16,) if name in ("dq", "sq") else (_F32, _BF16)
            pr[name] = _project(xp_b, w_in_b, cos_p, sin_p, off, width, rope=rope, scale=sc,
                                out_dtypes=dts, tm_pref=1024)
        as3 = lambda a: a.reshape(bsz, seq, a.shape[1])
        o_d = _diff_attention(lam_p, g_sub, as3(pr["dq"][0]), as3(pr["dk"][1]), as3(pr["dv"][1]), lam_init=lam_init)
        o_s = _sb_attention(as3(pr["sq"][0]), as3(pr["sk"][1]), as3(pr["sv"][1]))
        outs["pdk"].append(pr["dk"][0].reshape(bsz, seq, h_diff, hw))
        outs["pdv"].append(pr["dv"][0].reshape(bsz, seq, h_diff, hw))
        outs["psk"].append(pr["sk"][0].reshape(bsz, seq, h_sb, HEAD_DIM))
        outs["psv"].append(pr["sv"][0].reshape(bsz, seq, h_sb, HEAD_DIM))
        h_f, h_b = _out_proj(o_d.reshape(bsz * seq, w_diff), o_s.reshape(bsz * seq, w_sb), xp, w_out_b,
                             *ln1, alpha=alpha, tm_pref=512)
        xp_next = _mlp(h_b, h_f, w_up_b, w_down_b, *ln2, alpha=alpha, tm_pref=512)

        xs_b = xs.astype(_BF16)
        sm = {}
        for name, (off, width, rope, sc) in sections.items():
            sm[name] = _project(xs_b, w_in_b, cos_s, sin_s, off, width, rope=rope, scale=sc,
                                out_dtypes=(_F32,), tm_pref=n_dec)[0]
        heads_d = lambda a: a.reshape(n_dec, h_diff, hw)
        od_s = _sample_diff_attention(page_table, lam_p, g_sub, heads_d(sm["dq"]), heads_d(sm["dk"]),
                                      heads_d(sm["dv"]), cache_diff_k, cache_diff_v, l, lam_init=lam_init)
        os_s = _sample_sb_attention(page_table, sm["sq"].reshape(n_dec, h_sb, HEAD_DIM), cache_sb_k, cache_sb_v, l)
        outs["sdk"].append(sm["dk"].reshape(n_dec, dec_seq, h_diff, hw))
        outs["sdv"].append(sm["dv"].reshape(n_dec, dec_seq, h_diff, hw))
        outs["ssk"].append(sm["sk"].reshape(n_dec, dec_seq, h_sb, HEAD_DIM))
        outs["ssv"].append(sm["sv"].reshape(n_dec, dec_seq, h_sb, HEAD_DIM))
        hs_f, hs_b = _out_proj(od_s.reshape(n_dec, w_diff).astype(_BF16), os_s.reshape(n_dec, w_sb).astype(_BF16),
                               xs, w_out_b, *ln1, alpha=alpha, tm_pref=n_dec)
        xs = _mlp(hs_b, hs_f, w_up_b, w_down_b, *ln2, alpha=alpha, tm_pref=n_dec)
        xp = xp_next

    stack = lambda k: jnp.stack(outs[k])
    return (xp.reshape(bsz, seq, d), xs.reshape(n_dec, dec_seq, d),
            stack("pdk"), stack("pdv"), stack("psk"), stack("psv"),
            stack("sdk"), stack("sdv"), stack("ssk"), stack("ssv"))
```

```python
import functools
import math

import jax
import jax.numpy as jnp
from jax import lax
from jax.experimental import pallas as pl
from jax.experimental.pallas import tpu as pltpu

HEAD_DIM = 128
ROPE_THETA = 10000.0
LN_EPS = 1e-5
SUBLN_EPS = 1e-5
EXP_UNDERFLOW = -104.0
LOG2_E = 1.0 / math.log(2.0)

V7X_LANES = 128
V7X_VMEM_BYTES = 64 * 1024 * 1024
V7X_VMEM_REQUEST_CAP = V7X_VMEM_BYTES - 4 * 1024 * 1024

_F32 = jnp.float32
_BF16 = jnp.bfloat16
_NT = (((1,), (1,)), ((), ()))


def _lambda_init(layer):
    return 0.8 - 0.6 * math.exp(-0.3 * layer)


def _tile(dim, pref):
    t = min(dim, pref)
    assert dim % t == 0, (dim, pref)
    return t


def _params(semantics, vmem_estimate_bytes):
    limit = min(max(int(vmem_estimate_bytes * 1.2), 32 * 1024 * 1024), V7X_VMEM_REQUEST_CAP)
    return pltpu.CompilerParams(dimension_semantics=semantics, vmem_limit_bytes=limit)


def _proj_kernel(x_ref, w_ref, cos_ref, sin_ref, *out_refs, rope, scale):
    u = jnp.dot(x_ref[...], w_ref[...], preferred_element_type=_F32)
    tn = u.shape[1]
    for c in range(tn // HEAD_DIM):
        sl = slice(c * HEAD_DIM, (c + 1) * HEAD_DIM)
        uc = u[:, sl]
        if rope:
            uc = uc * cos_ref[...] + pltpu.roll(uc, HEAD_DIM // 2, 1) * sin_ref[...]
        if scale != 1.0:
            uc = uc * scale
        for o_ref in out_refs:
            o_ref[:, sl] = uc.astype(o_ref.dtype)


def _project(x_b, w_b, cos, sin, col_off, n_cols, *, rope, scale, out_dtypes, tm_pref):
    m, d = x_b.shape
    tm = _tile(cos.shape[0], tm_pref)
    tn = _tile(n_cols, 512)
    assert col_off % tn == 0 and m % tm == 0
    pos_blocks = cos.shape[0] // tm
    est = 2 * (tm * d * 2 + d * tn * 2 + 2 * tm * HEAD_DIM * 4) + tm * tn * 4
    est += sum(2 * tm * tn * jnp.dtype(dt).itemsize for dt in out_dtypes)
    return pl.pallas_call(
        functools.partial(_proj_kernel, rope=rope, scale=scale),
        grid=(m // tm, n_cols // tn),
        in_specs=[
            pl.BlockSpec((tm, d), lambda i, j: (i, 0)),
            pl.BlockSpec((d, tn), lambda i, j: (0, col_off // tn + j)),
            pl.BlockSpec((tm, HEAD_DIM), lambda i, j: (i % pos_blocks, 0)),
            pl.BlockSpec((tm, HEAD_DIM), lambda i, j: (i % pos_blocks, 0)),
        ],
        out_specs=[pl.BlockSpec((tm, tn), lambda i, j: (i, j)) for _ in out_dtypes],
        out_shape=[jax.ShapeDtypeStruct((m, n_cols), dt) for dt in out_dtypes],
        compiler_params=_params(("parallel", "arbitrary"), est),
        name="proj",
    )(x_b, w_b, cos, sin)


def _lam_from_params(lam_ref, lam_init):
    p = lam_ref[...]
    a = jnp.sum(p[0:1] * p[1:2], axis=1, keepdims=True)
    b = jnp.sum(p[2:3] * p[3:4], axis=1, keepdims=True)
    return jnp.exp(a) - jnp.exp(b) + lam_init


def _diff_attn_kernel(lam_ref, g_ref, q_ref, k_ref, v_ref, o_ref, acc1_ref, acc2_ref, *, t, lam_init):
    i = pl.program_id(2)
    q = q_ref[0]
    qs = (q[:, :HEAD_DIM], q[:, HEAD_DIM:])
    accs = (acc1_ref, acc2_ref)
    acc1_ref[...] = jnp.zeros_like(acc1_ref)
    acc2_ref[...] = jnp.zeros_like(acc2_ref)

    def block(kb, carry, masked):
        start = pl.multiple_of(kb * t, t)
        k = k_ref[0, pl.ds(start, t), :]
        v = v_ref[0, pl.ds(start, t), :]
        new = []
        for c in range(2):
            m, l = carry[2 * c], carry[2 * c + 1]
            s = lax.dot_general(qs[c], k[:, c * HEAD_DIM:(c + 1) * HEAD_DIM], _NT,
                                preferred_element_type=_F32)
            if masked:
                row = lax.broadcasted_iota(jnp.int32, s.shape, 0)
                col = lax.broadcasted_iota(jnp.int32, s.shape, 1)
                s = jnp.where(col <= row, s, -jnp.inf)
            m_new = jnp.maximum(m, jnp.max(s, axis=1, keepdims=True))
            p = jnp.exp2(s - m_new)
            alpha = jnp.exp2(m - m_new)
            l_new = alpha * l + jnp.sum(p, axis=1, keepdims=True)
            accs[c][...] = alpha * accs[c][...] + jnp.dot(p.astype(_BF16), v, preferred_element_type=_F32)
            new += [m_new, l_new]
        return tuple(new)

    neg = jnp.full((t, 1), -jnp.inf, _F32)
    zero = jnp.zeros((t, 1), _F32)
    carry = lax.fori_loop(0, i, lambda kb, c: block(kb, c, False), (neg, zero, neg, zero))
    _, l1, _, l2 = block(i, carry, True)

    lam = _lam_from_params(lam_ref, lam_init)
    o = acc1_ref[...] / l1 - lam * (acc2_ref[...] / l2)
    ms = jnp.mean(o * o, axis=1, keepdims=True)
    o_ref[0] = (o * lax.rsqrt(ms + SUBLN_EPS) * g_ref[...] * (1.0 - lam_init)).astype(o_ref.dtype)


def _diff_attention(lam_p, g, q, k, v, *, lam_init):
    b, s, w = q.shape
    hw = 2 * HEAD_DIM
    t = _tile(s, 512)
    est = 2 * (2 * t * hw * 2 + 2 * s * hw * 2) + 2 * t * hw * 4 + 8 * t * t * 4
    return pl.pallas_call(
        functools.partial(_diff_attn_kernel, t=t, lam_init=lam_init),
        grid=(b, w // hw, s // t),
        in_specs=[
            pl.BlockSpec((4, HEAD_DIM), lambda bi, h, i: (0, 0)),
            pl.BlockSpec((1, hw), lambda bi, h, i: (0, 0)),
            pl.BlockSpec((1, t, hw), lambda bi, h, i: (bi, i, h)),
            pl.BlockSpec((1, s, hw), lambda bi, h, i: (bi, 0, h)),
            pl.BlockSpec((1, s, hw), lambda bi, h, i: (bi, 0, h)),
        ],
        out_specs=pl.BlockSpec((1, t, hw), lambda bi, h, i: (bi, i, h)),
        out_shape=jax.ShapeDtypeStruct((b, s, w), _BF16),
        scratch_shapes=[pltpu.VMEM((t, hw), _F32), pltpu.VMEM((t, hw), _F32)],
        compiler_params=_params(("parallel", "parallel", "arbitrary"), est),
        name="diff_attn",
    )(lam_p, g, q, k, v)


def _suffix_sum_matrix(n):
    j = lax.broadcasted_iota(jnp.int32, (n, n), 0)
    s = lax.broadcasted_iota(jnp.int32, (n, n), 1)
    return (j >= s).astype(_BF16)


def _log_one_minus_beta(z):
    return -(jnp.maximum(z, 0.0) + jnp.log(1.0 + jnp.exp(-jnp.abs(z))))


def _suffix_sums(log_1m, tmat):
    hi = log_1m.astype(_BF16)
    lo = (log_1m - hi.astype(_F32)).astype(_BF16)
    return (jnp.dot(hi, tmat, preferred_element_type=_F32)
            + jnp.dot(lo, tmat, preferred_element_type=_F32))


SB_HEADS_PER_STEP = 4


def _sb_attn_kernel(q_ref, k_ref, v_ref, o_ref, acc_ref, *, t, g):
    i = pl.program_id(2)
    tmat = _suffix_sum_matrix(t)
    tmat2 = jnp.concatenate([tmat, tmat], axis=0)

    def block(kb, carries, masked):
        start = pl.multiple_of(kb * t, t)
        if masked:
            row = lax.broadcasted_iota(jnp.int32, (t, t), 0)
            col = lax.broadcasted_iota(jnp.int32, (t, t), 1)
            valid = col < row
        new = []
        for h in range(g):
            sl = slice(h * HEAD_DIM, (h + 1) * HEAD_DIM)
            z = lax.dot_general(q_ref[0, :, sl], k_ref[0, pl.ds(start, t), sl], _NT,
                                preferred_element_type=_F32)
            log_1m = _log_one_minus_beta(z)
            if masked:
                log_1m = jnp.where(valid, log_1m, 0.0)
            hi = log_1m.astype(_BF16)
            lo = (log_1m - hi.astype(_F32)).astype(_BF16)
            incl = jnp.dot(jnp.concatenate([hi, lo], axis=1), tmat2, preferred_element_type=_F32)
            w = jnp.exp(z + incl + carries[h])
            if masked:
                w = jnp.where(valid, w, 0.0)
            acc_ref[:, sl] += jnp.dot(w.astype(_BF16), v_ref[0, pl.ds(start, t), sl], preferred_element_type=_F32)
            new.append(carries[h] + jnp.sum(log_1m, axis=1, keepdims=True))
        return tuple(new)

    def live(carries):
        return functools.reduce(jnp.maximum, [jnp.max(c) for c in carries]) > EXP_UNDERFLOW

    acc_ref[...] = jnp.zeros_like(acc_ref)
    carries = block(i, (jnp.zeros((t, 1), _F32),) * g, True)

    def cond(state):
        kb, _, alive = state
        return jnp.logical_and(kb >= 0, alive)

    def body(state):
        kb, carries, _ = state
        carries = block(kb, carries, False)
        return kb - 1, carries, live(carries)

    lax.while_loop(cond, body, (i - 1, carries, live(carries)))
    o_ref[0] = acc_ref[...].astype(o_ref.dtype)


def _sb_attention(q, k, v):
    b, s, w = q.shape
    t = _tile(s, 256)
    g = _tile(w // HEAD_DIM, SB_HEADS_PER_STEP)
    gw = g * HEAD_DIM
    est = 2 * (2 * t * gw * 2 + 2 * s * gw * 2) + t * gw * 4 + g * 10 * t * t * 4
    return pl.pallas_call(
        functools.partial(_sb_attn_kernel, t=t, g=g),
        grid=(b, w // gw, s // t),
        in_specs=[
            pl.BlockSpec((1, t, gw), lambda bi, h, i: (bi, i, h)),
            pl.BlockSpec((1, s, gw), lambda bi, h, i: (bi, 0, h)),
            pl.BlockSpec((1, s, gw), lambda bi, h, i: (bi, 0, h)),
        ],
        out_specs=pl.BlockSpec((1, t, gw), lambda bi, h, i: (bi, i, h)),
        out_shape=jax.ShapeDtypeStruct((b, s, w), _BF16),
        scratch_shapes=[pltpu.VMEM((t, gw), _F32)],
        compiler_params=_params(("parallel", "parallel", "arbitrary"), est),
        name="sb_attn",
    )(q, k, v)


PAGES_PER_STEP = 8


def _own_head_mask(n_rows, n_cols, n_heads):
    row = lax.broadcasted_iota(jnp.int32, (n_rows, n_cols), 0)
    col = lax.broadcasted_iota(jnp.int32, (n_rows, n_cols), 1)
    return col % n_heads == row % n_heads


def _sample_diff_kernel(pt_ref, lam_ref, g_ref, q_ref, knew_ref, vnew_ref, *refs, pp, lam_init):
    k_pages, v_pages = refs[:pp], refs[pp:2 * pp]
    o_ref, qm_ref, m_ref, l_ref, acc_ref = refs[2 * pp:]
    del pt_ref
    p = pl.program_id(1)
    page, nh, hw = k_pages[0].shape
    twice = lambda a: jnp.concatenate([a, a], axis=0)

    @pl.when(p == 0)
    def _():
        row = lax.broadcasted_iota(jnp.int32, (2 * nh, hw), 0)
        lane = lax.broadcasted_iota(jnp.int32, (2 * nh, hw), 1)
        qm_ref[...] = jnp.where(lane // HEAD_DIM == row // nh, twice(q_ref[...]), 0.0)
        m_ref[...] = jnp.full_like(m_ref, -jnp.inf)
        l_ref[...] = jnp.zeros_like(l_ref)
        acc_ref[...] = jnp.zeros_like(acc_ref)

    def update(s, pv_fn):
        m = m_ref[...]
        m_new = jnp.maximum(m, jnp.max(s, axis=1, keepdims=True))
        pr = jnp.exp2(s - m_new)
        alpha = jnp.exp2(m - m_new)
        l_ref[...] = alpha * l_ref[...] + jnp.sum(pr, axis=1, keepdims=True)
        acc_ref[...] = alpha * acc_ref[...] + pv_fn(pr)
        m_ref[...] = m_new

    qm = qm_ref[...].astype(_BF16)
    own = _own_head_mask(2 * nh, page * nh, nh)
    for slot in range(pp):
        kp = k_pages[slot][...].reshape(page * nh, hw).astype(_BF16)
        vp = v_pages[slot][...].reshape(page * nh, hw).astype(_BF16)
        s = lax.dot_general(qm, kp, _NT, preferred_element_type=_F32)
        s = jnp.where(own, s, -jnp.inf)
        update(s, lambda pr, vp=vp: jnp.dot(pr.astype(_BF16), vp, preferred_element_type=_F32))

    @pl.when(p == pl.num_programs(1) - 1)
    def _():
        s_new = jnp.sum(qm_ref[...] * twice(knew_ref[...]), axis=1, keepdims=True)
        update(s_new, lambda pr: pr * twice(vnew_ref[...]))
        lam = _lam_from_params(lam_ref, lam_init)
        norm = acc_ref[...] / l_ref[...]
        o = norm[:nh] - lam * norm[nh:]
        ms = jnp.mean(o * o, axis=1, keepdims=True)
        o_ref[...] = o * lax.rsqrt(ms + SUBLN_EPS) * g_ref[...] * (1.0 - lam_init)


def _sample_sb_kernel(pt_ref, q_ref, ck_ref, cv_ref, o_ref, kbuf, vbuf, sem, acc_ref, *, layer, n_pages):
    b = pl.program_id(0)
    _, page, nh, d = kbuf.shape
    n_cols = page * nh

    def copies(j, slot):
        src = pt_ref[b, n_pages - 1 - j]
        return (pltpu.make_async_copy(ck_ref.at[layer, src], kbuf.at[slot], sem.at[0, slot]),
                pltpu.make_async_copy(cv_ref.at[layer, src], vbuf.at[slot], sem.at[1, slot]))

    def start(j, slot):
        for c in copies(j, slot):
            c.start()

    def wait(j, slot):
        for c in copies(j, slot):
            c.wait()

    acc_ref[...] = jnp.zeros_like(acc_ref)
    start(0, 0)
    q = q_ref[...].astype(_BF16)
    own = _own_head_mask(nh, n_cols, nh)
    tmat = _suffix_sum_matrix(V7X_LANES)

    def cond(state):
        j, _, alive = state
        return jnp.logical_and(j < n_pages, alive)

    def body(state):
        j, carry, _ = state
        slot = lax.rem(j, 2)
        wait(j, slot)

        @pl.when(j + 1 < n_pages)
        def _():
            start(j + 1, 1 - slot)

        kp = kbuf[slot].reshape(n_cols, d).astype(_BF16)
        vp = vbuf[slot].reshape(n_cols, d).astype(_BF16)
        z = lax.dot_general(q, kp, _NT, preferred_element_type=_F32)
        log_1m = jnp.where(own, _log_one_minus_beta(z), 0.0)
        running = carry
        w_tiles = []
        for c in reversed(range(n_cols // V7X_LANES)):
            sl = slice(c * V7X_LANES, (c + 1) * V7X_LANES)
            incl = _suffix_sums(log_1m[:, sl], tmat)
            w_tiles.append(jnp.where(own[:, sl], jnp.exp(z[:, sl] + incl + running), 0.0))
            running = running + jnp.sum(log_1m[:, sl], axis=1, keepdims=True)
        w = jnp.concatenate(w_tiles[::-1], axis=1)
        acc_ref[...] += jnp.dot(w.astype(_BF16), vp, preferred_element_type=_F32)
        return j + 1, running, jnp.max(running) > EXP_UNDERFLOW

    j_end, _, _ = lax.while_loop(cond, body, (jnp.int32(0), jnp.zeros((nh, 1), _F32), jnp.bool_(True)))

    @pl.when(j_end < n_pages)
    def _():
        wait(j_end, lax.rem(j_end, 2))

    o_ref[...] = acc_ref[...]


def _page_specs(page_shape, layer, pp):
    def spec(slot):
        return pl.BlockSpec((None, None) + tuple(page_shape),
                            lambda b, p, pt: (layer, pt[b, p * pp + slot], 0, 0, 0))
    return [spec(slot) for slot in range(pp)]


def _sample_diff_attention(page_table, lam_p, g, q, k_new, v_new, cache_k, cache_v, layer, *, lam_init):
    n, nh, hw = q.shape
    n_pages = page_table.shape[1]
    page_shape = cache_k.shape[2:]
    pp = _tile(n_pages, PAGES_PER_STEP)
    head_spec = pl.BlockSpec((None, nh, hw), lambda b, p, pt: (b, 0, 0))
    pages = _page_specs(page_shape, layer, pp)
    page_elems = math.prod(page_shape)
    est = 2 * 2 * pp * page_elems * 4 + 4 * page_elems * 2 + 4 * 2 * nh * page_shape[0] * nh * 4
    return pl.pallas_call(
        functools.partial(_sample_diff_kernel, pp=pp, lam_init=lam_init),
        grid_spec=pltpu.PrefetchScalarGridSpec(
            num_scalar_prefetch=1,
            grid=(n, n_pages // pp),
            in_specs=[
                pl.BlockSpec((4, HEAD_DIM), lambda b, p, pt: (0, 0)),
                pl.BlockSpec((1, hw), lambda b, p, pt: (0, 0)),
                head_spec, head_spec, head_spec,
            ] + pages + pages,
            out_specs=head_spec,
            scratch_shapes=[pltpu.VMEM((2 * nh, hw), _F32), pltpu.VMEM((2 * nh, 1), _F32),
                            pltpu.VMEM((2 * nh, 1), _F32), pltpu.VMEM((2 * nh, hw), _F32)],
        ),
        out_shape=jax.ShapeDtypeStruct((n, nh, hw), _F32),
        compiler_params=_params(("parallel", "arbitrary"), est),
        name="sample_diff_attn",
    )(page_table, lam_p, g, q, k_new, v_new, *([cache_k] * pp), *([cache_v] * pp))


def _sample_sb_attention(page_table, q, cache_k, cache_v, layer):
    n, nh, d = q.shape
    n_pages = page_table.shape[1]
    page_shape = tuple(cache_k.shape[2:])
    head_spec = pl.BlockSpec((None, nh, d), lambda b, pt: (b, 0, 0))
    page_elems = math.prod(page_shape)
    est = 2 * 2 * page_elems * 4 + 4 * page_elems * 2 + 8 * nh * page_shape[0] * nh * 4
    return pl.pallas_call(
        functools.partial(_sample_sb_kernel, layer=layer, n_pages=n_pages),
        grid_spec=pltpu.PrefetchScalarGridSpec(
            num_scalar_prefetch=1,
            grid=(n,),
            in_specs=[head_spec, pl.BlockSpec(memory_space=pl.ANY), pl.BlockSpec(memory_space=pl.ANY)],
            out_specs=head_spec,
            scratch_shapes=[pltpu.VMEM((2,) + page_shape, _F32), pltpu.VMEM((2,) + page_shape, _F32),
                            pltpu.SemaphoreType.DMA((2, 2)), pltpu.VMEM((nh, d), _F32)],
        ),
        out_shape=jax.ShapeDtypeStruct((n, nh, d), _F32),
        compiler_params=_params(("arbitrary",), est),
        name="sample_sb_attn",
    )(page_table, q, cache_k, cache_v)


def _layer_norm_rows(z, g, b):
    mu = jnp.mean(z, axis=1, keepdims=True)
    zc = z - mu
    var = jnp.mean(zc * zc, axis=1, keepdims=True)
    return zc * lax.rsqrt(var + LN_EPS) * g + b


def _out_proj_kernel(od_ref, os_ref, x_ref, w_ref, g_ref, b_ref, hf_ref, hb_ref, *, alpha, tn):
    n = pl.program_id(1)
    wd = od_ref.shape[1]
    z = alpha * x_ref[...]
    z += jnp.dot(od_ref[...], w_ref[:wd, :], preferred_element_type=_F32)
    z += jnp.dot(os_ref[...], w_ref[wd:, :], preferred_element_type=_F32)
    n_tiles = hf_ref.shape[1] // tn
    for nn in range(n_tiles):
        @pl.when(n == nn)
        def _(nn=nn):
            hf_ref[:, nn * tn:(nn + 1) * tn] = z

    @pl.when(n == n_tiles - 1)
    def _():
        h = _layer_norm_rows(hf_ref[...], g_ref[...], b_ref[...])
        hf_ref[...] = h
        hb_ref[...] = h.astype(hb_ref.dtype)


def _out_proj(o_d, o_s, x, w_b, g, b, *, alpha, tm_pref):
    m, d = x.shape
    wd, ws = o_d.shape[1], o_s.shape[1]
    tm = _tile(m, tm_pref)
    tn = _tile(d, 512)
    est = 2 * (tm * (wd + ws) * 2 + tm * tn * 4 + (wd + ws) * tn * 2 + tm * d * 6) + 2 * tm * d * 4
    return pl.pallas_call(
        functools.partial(_out_proj_kernel, alpha=alpha, tn=tn),
        grid=(m // tm, d // tn),
        in_specs=[
            pl.BlockSpec((tm, wd), lambda i, n: (i, 0)),
            pl.BlockSpec((tm, ws), lambda i, n: (i, 0)),
            pl.BlockSpec((tm, tn), lambda i, n: (i, n)),
            pl.BlockSpec((wd + ws, tn), lambda i, n: (0, n)),
            pl.BlockSpec((1, d), lambda i, n: (0, 0)),
            pl.BlockSpec((1, d), lambda i, n: (0, 0)),
        ],
        out_specs=[pl.BlockSpec((tm, d), lambda i, n: (i, 0)), pl.BlockSpec((tm, d), lambda i, n: (i, 0))],
        out_shape=[jax.ShapeDtypeStruct((m, d), _F32), jax.ShapeDtypeStruct((m, d), _BF16)],
        compiler_params=_params(("parallel", "arbitrary"), est),
        name="out_proj_ln",
    )(o_d, o_s, x, w_b, g, b)


def _mlp_kernel(hb_ref, hf_ref, wu_ref, wd_ref, g_ref, b_ref, y_ref, *, alpha):
    j = pl.program_id(1)

    @pl.when(j == 0)
    def _():
        y_ref[...] = alpha * hf_ref[...]

    a = jnp.maximum(jnp.dot(hb_ref[...], wu_ref[...], preferred_element_type=_F32), 0.0)
    y_ref[...] += jnp.dot((a * a).astype(_BF16), wd_ref[...], preferred_element_type=_F32)

    @pl.when(j == pl.num_programs(1) - 1)
    def _():
        y_ref[...] = _layer_norm_rows(y_ref[...], g_ref[...], b_ref[...])


def _mlp(h_b, h_f, wu_b, wd_b, g, b, *, alpha, tm_pref):
    m, d = h_f.shape
    dff = wu_b.shape[1]
    tm = _tile(m, tm_pref)
    tf = _tile(dff, 512)
    est = 2 * tm * d * 2 + tm * d * 4 + 2 * 2 * d * tf * 2 + tm * d * 4 + tm * tf * 6 + tm * d * 4
    single = pl.Buffered(1)
    return pl.pallas_call(
        functools.partial(_mlp_kernel, alpha=alpha),
        grid=(m // tm, dff // tf),
        in_specs=[
            pl.BlockSpec((tm, d), lambda i, j: (i, 0)),
            pl.BlockSpec((tm, d), lambda i, j: (i, 0), pipeline_mode=single),
            pl.BlockSpec((d, tf), lambda i, j: (0, j)),
            pl.BlockSpec((tf, d), lambda i, j: (j, 0)),
            pl.BlockSpec((1, d), lambda i, j: (0, 0)),
            pl.BlockSpec((1, d), lambda i, j: (0, 0)),
        ],
        out_specs=pl.BlockSpec((tm, d), lambda i, j: (i, 0), pipeline_mode=single),
        out_shape=jax.ShapeDtypeStruct((m, d), _F32),
        compiler_params=_params(("parallel", "arbitrary"), est),
        name="mlp_ln",
    )(h_b, h_f, wu_b, wd_b, g, b)


def _rope_tables(pos):
    inv = ROPE_THETA ** (-jnp.arange(0, HEAD_DIM, 2, dtype=_F32) / HEAD_DIM)
    ang = pos.astype(_F32)[:, None] * inv[None, :]
    cos, sin = jnp.cos(ang), jnp.sin(ang)
    return jnp.concatenate([cos, cos], axis=1), jnp.concatenate([-sin, sin], axis=1)


def kernel(x_prompt, x_sample, cache_diff_k, cache_diff_v, cache_sb_k, cache_sb_v, page_table, w_in, lambda_q1, lambda_k1, lambda_q2, lambda_k2, subln_g, w_out, ln1_g, ln1_b, w_up, w_down, ln2_g, ln2_b):
    bsz, seq, d = x_prompt.shape
    n_dec, dec_seq, _ = x_sample.shape
    assert dec_seq == 1
    depth, n_pool, page, h_diff, hw = cache_diff_k.shape
    h_sb = cache_sb_k.shape[3]
    assert hw == 2 * HEAD_DIM and cache_sb_k.shape[4] == HEAD_DIM
    w_diff, w_sb = h_diff * hw, h_sb * HEAD_DIM
    past_len = page_table.shape[1] * page
    alpha = (2.0 * depth) ** 0.25
    scale = HEAD_DIM ** -0.5

    cos_p, sin_p = _rope_tables(jnp.arange(seq, dtype=jnp.int32))
    cos_s, sin_s = _rope_tables(jnp.full((n_dec,), past_len, dtype=jnp.int32))

    sections = {
        "dq": (0, w_diff, True, scale * LOG2_E), "dk": (w_diff, w_diff, True, 1.0), "dv": (2 * w_diff, w_diff, False, 1.0),
        "sq": (3 * w_diff, w_sb, False, scale), "sk": (3 * w_diff + w_sb, w_sb, False, 1.0),
        "sv": (3 * w_diff + 2 * w_sb, w_sb, False, 1.0),
    }

    xp = x_prompt.reshape(bsz * seq, d)
    xs = x_sample.reshape(n_dec, d)
    outs = {k: [] for k in ("pdk", "pdv", "psk", "psv", "sdk", "sdv", "ssk", "ssv")}
    for l in range(depth):
        lam_init = _lambda_init(l)
        lam_p = jnp.stack([lambda_q1[l], lambda_k1[l], lambda_q2[l], lambda_k2[l]]).astype(_F32)
        g_sub = subln_g[l].reshape(1, hw).astype(_F32)
        w_in_b, w_out_b = w_in[l].astype(_BF16), w_out[l].astype(_BF16)
        w_up_b, w_down_b = w_up[l].astype(_BF16), w_down[l].astype(_BF16)
        ln1 = (ln1_g[l].reshape(1, d), ln1_b[l].reshape(1, d))
        ln2 = (ln2_g[l].reshape(1, d), ln2_b[l].reshape(1, d))

        xp_b = xp.astype(_BF16)
        pr = {}
        for name, (off, width, rope, sc) in sections.items():
            dts = (_BF16,) if name in ("dq", "sq") else (_F32, _BF16)
            pr[name] = _project(xp_b, w_in_b, cos_p, sin_p, off, width, rope=rope, scale=sc,
                                out_dtypes=dts, tm_pref=1024)
        as3 = lambda a: a.reshape(bsz, seq, a.shape[1])
        o_d = _diff_attention(lam_p, g_sub, as3(pr["dq"][0]), as3(pr["dk"][1]), as3(pr["dv"][1]), lam_init=lam_init)
        o_s = _sb_attention(as3(pr["sq"][0]), as3(pr["sk"][1]), as3(pr["sv"][1]))
        outs["pdk"].append(pr["dk"][0].reshape(bsz, seq, h_diff, hw))
        outs["pdv"].append(pr["dv"][0].reshape(bsz, seq, h_diff, hw))
        outs["psk"].append(pr["sk"][0].reshape(bsz, seq, h_sb, HEAD_DIM))
        outs["psv"].append(pr["sv"][0].reshape(bsz, seq, h_sb, HEAD_DIM))
        h_f, h_b = _out_proj(o_d.reshape(bsz * seq, w_diff), o_s.reshape(bsz * seq, w_sb), xp, w_out_b,
                             *ln1, alpha=alpha, tm_pref=512)
        xp_next = _mlp(h_b, h_f, w_up_b, w_down_b, *ln2, alpha=alpha, tm_pref=512)

        xs_b = xs.astype(_BF16)
        sm = {}
        for name, (off, width, rope, sc) in sections.items():
            sm[name] = _project(xs_b, w_in_b, cos_s, sin_s, off, width, rope=rope, scale=sc,
                                out_dtypes=(_F32,), tm_pref=n_dec)[0]
        heads_d = lambda a: a.reshape(n_dec, h_diff, hw)
        od_s = _sample_diff_attention(page_table, lam_p, g_sub, heads_d(sm["dq"]), heads_d(sm["dk"]),
                                      heads_d(sm["dv"]), cache_diff_k, cache_diff_v, l, lam_init=lam_init)
        os_s = _sample_sb_attention(page_table, sm["sq"].reshape(n_dec, h_sb, HEAD_DIM), cache_sb_k, cache_sb_v, l)
        outs["sdk"].append(sm["dk"].reshape(n_dec, dec_seq, h_diff, hw))
        outs["sdv"].append(sm["dv"].reshape(n_dec, dec_seq, h_diff, hw))
        outs["ssk"].append(sm["sk"].reshape(n_dec, dec_seq, h_sb, HEAD_DIM))
        outs["ssv"].append(sm["sv"].reshape(n_dec, dec_seq, h_sb, HEAD_DIM))
        hs_f, hs_b = _out_proj(od_s.reshape(n_dec, w_diff).astype(_BF16), os_s.reshape(n_dec, w_sb).astype(_BF16),
                               xs, w_out_b, *ln1, alpha=alpha, tm_pref=n_dec)
        xs = _mlp(hs_b, hs_f, w_up_b, w_down_b, *ln2, alpha=alpha, tm_pref=n_dec)
        xp = xp_next

    stack = lambda k: jnp.stack(outs[k])
    return (xp.reshape(bsz, seq, d), xs.reshape(n_dec, dec_seq, d),
            stack("pdk"), stack("pdv"), stack("psk"), stack("psv"),
            stack("sdk"), stack("sdv"), stack("ssk"), stack("ssv"))
```

```python
import functools
import math

import jax
import jax.numpy as jnp
from jax import lax
from jax.experimental import pallas as pl
from jax.experimental.pallas import tpu as pltpu

HEAD_DIM = 128
ROPE_THETA = 10000.0
LN_EPS = 1e-5
SUBLN_EPS = 1e-5
EXP_UNDERFLOW = -104.0
LOG2_E = 1.0 / math.log(2.0)

V7X_LANES = 128
V7X_VMEM_BYTES = 64 * 1024 * 1024
V7X_VMEM_REQUEST_CAP = V7X_VMEM_BYTES - 4 * 1024 * 1024

_F32 = jnp.float32
_BF16 = jnp.bfloat16
_NT = (((1,), (1,)), ((), ()))


def _lambda_init(layer):
    return 0.8 - 0.6 * math.exp(-0.3 * layer)


def _tile(dim, pref):
    t = min(dim, pref)
    assert dim % t == 0, (dim, pref)
    return t


def _params(semantics, vmem_estimate_bytes):
    limit = min(max(int(vmem_estimate_bytes * 1.2), 32 * 1024 * 1024), V7X_VMEM_REQUEST_CAP)
    return pltpu.CompilerParams(dimension_semantics=semantics, vmem_limit_bytes=limit)


def _proj_kernel(x_ref, w_ref, cos_ref, sin_ref, *out_refs, rope, scale):
    u = jnp.dot(x_ref[...], w_ref[...], preferred_element_type=_F32)
    tn = u.shape[1]
    for c in range(tn // HEAD_DIM):
        sl = slice(c * HEAD_DIM, (c + 1) * HEAD_DIM)
        uc = u[:, sl]
        if rope:
            uc = uc * cos_ref[...] + pltpu.roll(uc, HEAD_DIM // 2, 1) * sin_ref[...]
        if scale != 1.0:
            uc = uc * scale
        for o_ref in out_refs:
            o_ref[:, sl] = uc.astype(o_ref.dtype)


def _project(x_b, w_b, cos, sin, col_off, n_cols, *, rope, scale, out_dtypes, tm_pref, tn_pref):
    m, d = x_b.shape
    tm = _tile(cos.shape[0], tm_pref)
    tn = _tile(n_cols, tn_pref)
    assert col_off % tn == 0 and m % tm == 0
    pos_blocks = cos.shape[0] // tm
    est = 2 * (tm * d * 2 + d * tn * 2 + 2 * tm * HEAD_DIM * 4) + tm * tn * 4
    est += sum(2 * tm * tn * jnp.dtype(dt).itemsize for dt in out_dtypes)
    return pl.pallas_call(
        functools.partial(_proj_kernel, rope=rope, scale=scale),
        grid=(m // tm, n_cols // tn),
        in_specs=[
            pl.BlockSpec((tm, d), lambda i, j: (i, 0)),
            pl.BlockSpec((d, tn), lambda i, j: (0, col_off // tn + j)),
            pl.BlockSpec((tm, HEAD_DIM), lambda i, j: (i % pos_blocks, 0)),
            pl.BlockSpec((tm, HEAD_DIM), lambda i, j: (i % pos_blocks, 0)),
        ],
        out_specs=[pl.BlockSpec((tm, tn), lambda i, j: (i, j)) for _ in out_dtypes],
        out_shape=[jax.ShapeDtypeStruct((m, n_cols), dt) for dt in out_dtypes],
        compiler_params=_params(("parallel", "arbitrary"), est),
        name="proj",
    )(x_b, w_b, cos, sin)


def _proj_rows_kernel(x_ref, w_ref, cos_ref, sin_ref, o_ref, *, rope_tiles, scale_tiles):
    j = pl.program_id(0)
    u = jnp.dot(x_ref[...], w_ref[...], preferred_element_type=_F32)
    is_rope = j < rope_tiles
    scale = jnp.float32(1.0)
    for lo, hi, s in scale_tiles:
        scale = jnp.where(jnp.logical_and(j >= lo, j < hi), jnp.float32(s), scale)
    for c in range(u.shape[1] // HEAD_DIM):
        sl = slice(c * HEAD_DIM, (c + 1) * HEAD_DIM)
        uc = u[:, sl]
        rot = uc * cos_ref[...] + pltpu.roll(uc, HEAD_DIM // 2, 1) * sin_ref[...]
        o_ref[:, sl] = jnp.where(is_rope, rot, uc) * scale


def _project_rows(x_b, w_b, cos, sin, rope_cols, scale_sections):
    r, d = x_b.shape
    n = w_b.shape[1]
    bounds = [rope_cols] + [c for lo, hi, _ in scale_sections for c in (lo, hi) if c]
    tn = _tile(functools.reduce(math.gcd, bounds, n), 1024)
    scale_tiles = tuple((lo // tn, hi // tn, s) for lo, hi, s in scale_sections)
    est = 2 * (r * d * 2 + d * tn * 2 + 2 * r * HEAD_DIM * 4 + r * tn * 4) + r * tn * 8
    return pl.pallas_call(
        functools.partial(_proj_rows_kernel, rope_tiles=rope_cols // tn, scale_tiles=scale_tiles),
        grid=(n // tn,),
        in_specs=[
            pl.BlockSpec((r, d), lambda j: (0, 0)),
            pl.BlockSpec((d, tn), lambda j: (0, j)),
            pl.BlockSpec((r, HEAD_DIM), lambda j: (0, 0)),
            pl.BlockSpec((r, HEAD_DIM), lambda j: (0, 0)),
        ],
        out_specs=pl.BlockSpec((r, tn), lambda j: (0, j)),
        out_shape=jax.ShapeDtypeStruct((r, n), _F32),
        compiler_params=_params(("arbitrary",), est),
        name="proj_rows",
    )(x_b, w_b, cos, sin)


def _lam_from_params(lam_ref, lam_init):
    p = lam_ref[...]
    a = jnp.sum(p[0:1] * p[1:2], axis=1, keepdims=True)
    b = jnp.sum(p[2:3] * p[3:4], axis=1, keepdims=True)
    return jnp.exp(a) - jnp.exp(b) + lam_init


def _diff_attn_kernel(lam_ref, g_ref, q_ref, k_ref, v_ref, o_ref, acc1_ref, acc2_ref, *, t, lam_init):
    i = pl.program_id(2)
    q = q_ref[0]
    qs = (q[:, :HEAD_DIM], q[:, HEAD_DIM:])
    accs = (acc1_ref, acc2_ref)
    acc1_ref[...] = jnp.zeros_like(acc1_ref)
    acc2_ref[...] = jnp.zeros_like(acc2_ref)

    def block(kb, carry, masked):
        start = pl.multiple_of(kb * t, t)
        k = k_ref[0, pl.ds(start, t), :]
        v = v_ref[0, pl.ds(start, t), :]
        ss = [lax.dot_general(qs[c], k[:, c * HEAD_DIM:(c + 1) * HEAD_DIM], _NT, preferred_element_type=_F32)
              for c in range(2)]
        new, ps, alphas = [], [], []
        for c in range(2):
            m, l = carry[2 * c], carry[2 * c + 1]
            s = ss[c]
            if masked:
                row = lax.broadcasted_iota(jnp.int32, s.shape, 0)
                col = lax.broadcasted_iota(jnp.int32, s.shape, 1)
                s = jnp.where(col <= row, s, -jnp.inf)
            m_new = jnp.maximum(m, jnp.max(s, axis=1, keepdims=True))
            p = jnp.exp2(s - m_new)
            alpha = jnp.exp2(m - m_new)
            new += [m_new, alpha * l + jnp.sum(p, axis=1, keepdims=True)]
            ps.append(p.astype(_BF16))
            alphas.append(alpha)
        pvs = [jnp.dot(ps[c], v, preferred_element_type=_F32) for c in range(2)]
        for c in range(2):
            accs[c][...] = alphas[c] * accs[c][...] + pvs[c]
        return tuple(new)

    neg = jnp.full((t, 1), -jnp.inf, _F32)
    zero = jnp.zeros((t, 1), _F32)
    carry = lax.fori_loop(0, i, lambda kb, c: block(kb, c, False), (neg, zero, neg, zero))
    _, l1, _, l2 = block(i, carry, True)

    lam = _lam_from_params(lam_ref, lam_init)
    o = acc1_ref[...] / l1 - lam * (acc2_ref[...] / l2)
    ms = jnp.mean(o * o, axis=1, keepdims=True)
    o_ref[0] = (o * lax.rsqrt(ms + SUBLN_EPS) * g_ref[...] * (1.0 - lam_init)).astype(o_ref.dtype)


def _diff_attention(lam_p, g, q, k, v, *, lam_init):
    b, s, w = q.shape
    hw = 2 * HEAD_DIM
    t = _tile(s, 512)
    est = 2 * (2 * t * hw * 2 + 2 * s * hw * 2) + 2 * t * hw * 4 + 8 * t * t * 4
    return pl.pallas_call(
        functools.partial(_diff_attn_kernel, t=t, lam_init=lam_init),
        grid=(b, w // hw, s // t),
        in_specs=[
            pl.BlockSpec((4, HEAD_DIM), lambda bi, h, i: (0, 0)),
            pl.BlockSpec((1, hw), lambda bi, h, i: (0, 0)),
            pl.BlockSpec((1, t, hw), lambda bi, h, i: (bi, i, h)),
            pl.BlockSpec((1, s, hw), lambda bi, h, i: (bi, 0, h)),
            pl.BlockSpec((1, s, hw), lambda bi, h, i: (bi, 0, h)),
        ],
        out_specs=pl.BlockSpec((1, t, hw), lambda bi, h, i: (bi, i, h)),
        out_shape=jax.ShapeDtypeStruct((b, s, w), _BF16),
        scratch_shapes=[pltpu.VMEM((t, hw), _F32), pltpu.VMEM((t, hw), _F32)],
        compiler_params=_params(("parallel", "parallel", "arbitrary"), est),
        name="diff_attn",
    )(lam_p, g, q, k, v)


def _suffix_sum_matrix(n):
    j = lax.broadcasted_iota(jnp.int32, (n, n), 0)
    s = lax.broadcasted_iota(jnp.int32, (n, n), 1)
    return (j >= s).astype(_BF16)


def _log_one_minus_beta(z):
    return -(jnp.maximum(z, 0.0) + jnp.log(1.0 + jnp.exp(-jnp.abs(z))))


def _suffix_sums(log_1m, tmat):
    hi = log_1m.astype(_BF16)
    lo = (log_1m - hi.astype(_F32)).astype(_BF16)
    return (jnp.dot(hi, tmat, preferred_element_type=_F32)
            + jnp.dot(lo, tmat, preferred_element_type=_F32))


SB_HEADS_PER_STEP = 4


def _sb_attn_kernel(q_ref, k_ref, v_ref, o_ref, acc_ref, *, t, g):
    i = pl.program_id(2)
    tmat = _suffix_sum_matrix(t)
    tmat2 = jnp.concatenate([tmat, tmat], axis=0)

    def block(kb, carries, masked):
        start = pl.multiple_of(kb * t, t)
        if masked:
            row = lax.broadcasted_iota(jnp.int32, (t, t), 0)
            col = lax.broadcasted_iota(jnp.int32, (t, t), 1)
            valid = col < row
        heads = [slice(h * HEAD_DIM, (h + 1) * HEAD_DIM) for h in range(g)]
        zs = [lax.dot_general(q_ref[0, :, sl], k_ref[0, pl.ds(start, t), sl], _NT, preferred_element_type=_F32)
              for sl in heads]
        new, hilos = [], []
        for h in range(g):
            log_1m = _log_one_minus_beta(zs[h])
            if masked:
                log_1m = jnp.where(valid, log_1m, 0.0)
            hi = log_1m.astype(_BF16)
            lo = (log_1m - hi.astype(_F32)).astype(_BF16)
            hilos.append(jnp.concatenate([hi, lo], axis=1))
            new.append(carries[h] + jnp.sum(log_1m, axis=1, keepdims=True))
        incls = [jnp.dot(hl, tmat2, preferred_element_type=_F32) for hl in hilos]
        ws = []
        for h in range(g):
            w = jnp.exp(zs[h] + incls[h] + carries[h])
            if masked:
                w = jnp.where(valid, w, 0.0)
            ws.append(w.astype(_BF16))
        pvs = [jnp.dot(ws[h], v_ref[0, pl.ds(start, t), heads[h]], preferred_element_type=_F32) for h in range(g)]
        for h in range(g):
            acc_ref[:, heads[h]] += pvs[h]
        return tuple(new)

    def live(carries):
        return functools.reduce(jnp.maximum, [jnp.max(c) for c in carries]) > EXP_UNDERFLOW

    acc_ref[...] = jnp.zeros_like(acc_ref)
    carries = block(i, (jnp.zeros((t, 1), _F32),) * g, True)

    def cond(state):
        kb, _, alive = state
        return jnp.logical_and(kb >= 0, alive)

    def body(state):
        kb, carries, _ = state
        carries = block(kb, carries, False)
        return kb - 1, carries, live(carries)

    lax.while_loop(cond, body, (i - 1, carries, live(carries)))
    o_ref[0] = acc_ref[...].astype(o_ref.dtype)


def _sb_attention(q, k, v):
    b, s, w = q.shape
    t = _tile(s, 256)
    g = _tile(w // HEAD_DIM, SB_HEADS_PER_STEP)
    gw = g * HEAD_DIM
    est = 2 * (2 * t * gw * 2 + 2 * s * gw * 2) + t * gw * 4 + g * 10 * t * t * 4
    return pl.pallas_call(
        functools.partial(_sb_attn_kernel, t=t, g=g),
        grid=(b, w // gw, s // t),
        in_specs=[
            pl.BlockSpec((1, t, gw), lambda bi, h, i: (bi, i, h)),
            pl.BlockSpec((1, s, gw), lambda bi, h, i: (bi, 0, h)),
            pl.BlockSpec((1, s, gw), lambda bi, h, i: (bi, 0, h)),
        ],
        out_specs=pl.BlockSpec((1, t, gw), lambda bi, h, i: (bi, i, h)),
        out_shape=jax.ShapeDtypeStruct((b, s, w), _BF16),
        scratch_shapes=[pltpu.VMEM((t, gw), _F32)],
        compiler_params=_params(("parallel", "parallel", "arbitrary"), est),
        name="sb_attn",
    )(q, k, v)


PAGES_PER_STEP = 8


def _own_head_mask(n_rows, n_cols, n_heads):
    row = lax.broadcasted_iota(jnp.int32, (n_rows, n_cols), 0)
    col = lax.broadcasted_iota(jnp.int32, (n_rows, n_cols), 1)
    return col % n_heads == row % n_heads


def _sample_diff_kernel(pt_ref, lam_ref, g_ref, q_ref, knew_ref, vnew_ref, *refs, pp, lam_init):
    k_pages, v_pages = refs[:pp], refs[pp:2 * pp]
    o_ref, qm_ref, m_ref, l_ref, acc_ref = refs[2 * pp:]
    del pt_ref
    p = pl.program_id(1)
    page, nh, hw = k_pages[0].shape
    twice = lambda a: jnp.concatenate([a, a], axis=0)

    @pl.when(p == 0)
    def _():
        row = lax.broadcasted_iota(jnp.int32, (2 * nh, hw), 0)
        lane = lax.broadcasted_iota(jnp.int32, (2 * nh, hw), 1)
        qm_ref[...] = jnp.where(lane // HEAD_DIM == row // nh, twice(q_ref[...]), 0.0)
        m_ref[...] = jnp.full_like(m_ref, -jnp.inf)
        l_ref[...] = jnp.zeros_like(l_ref)
        acc_ref[...] = jnp.zeros_like(acc_ref)

    def update(s, pv_fn):
        m = m_ref[...]
        m_new = jnp.maximum(m, jnp.max(s, axis=1, keepdims=True))
        pr = jnp.exp2(s - m_new)
        alpha = jnp.exp2(m - m_new)
        l_ref[...] = alpha * l_ref[...] + jnp.sum(pr, axis=1, keepdims=True)
        acc_ref[...] = alpha * acc_ref[...] + pv_fn(pr)
        m_ref[...] = m_new

    qm = qm_ref[...].astype(_BF16)
    own = _own_head_mask(2 * nh, page * nh, nh)
    for slot in range(pp):
        kp = k_pages[slot][...].reshape(page * nh, hw).astype(_BF16)
        vp = v_pages[slot][...].reshape(page * nh, hw).astype(_BF16)
        s = lax.dot_general(qm, kp, _NT, preferred_element_type=_F32)
        s = jnp.where(own, s, -jnp.inf)
        update(s, lambda pr, vp=vp: jnp.dot(pr.astype(_BF16), vp, preferred_element_type=_F32))

    @pl.when(p == pl.num_programs(1) - 1)
    def _():
        s_new = jnp.sum(qm_ref[...] * twice(knew_ref[...]), axis=1, keepdims=True)
        update(s_new, lambda pr: pr * twice(vnew_ref[...]))
        lam = _lam_from_params(lam_ref, lam_init)
        norm = acc_ref[...] / l_ref[...]
        o = norm[:nh] - lam * norm[nh:]
        ms = jnp.mean(o * o, axis=1, keepdims=True)
        o_ref[...] = o * lax.rsqrt(ms + SUBLN_EPS) * g_ref[...] * (1.0 - lam_init)


def _sample_sb_kernel(pt_ref, q_ref, ck_ref, cv_ref, o_ref, kbuf, vbuf, sem, acc_ref, *, layer, n_pages):
    b = pl.program_id(0)
    _, page, nh, d = kbuf.shape
    n_cols = page * nh

    def copies(j, slot):
        src = pt_ref[b, n_pages - 1 - j]
        return (pltpu.make_async_copy(ck_ref.at[layer, src], kbuf.at[slot], sem.at[0, slot]),
                pltpu.make_async_copy(cv_ref.at[layer, src], vbuf.at[slot], sem.at[1, slot]))

    def start(j, slot):
        for c in copies(j, slot):
            c.start()

    def wait(j, slot):
        for c in copies(j, slot):
            c.wait()

    acc_ref[...] = jnp.zeros_like(acc_ref)
    start(0, 0)
    q = q_ref[...].astype(_BF16)
    own = _own_head_mask(nh, n_cols, nh)
    tmat = _suffix_sum_matrix(V7X_LANES)

    def cond(state):
        j, _, alive = state
        return jnp.logical_and(j < n_pages, alive)

    def body(state):
        j, carry, _ = state
        slot = lax.rem(j, 2)
        wait(j, slot)

        @pl.when(j + 1 < n_pages)
        def _():
            start(j + 1, 1 - slot)

        kp = kbuf[slot].reshape(n_cols, d).astype(_BF16)
        vp = vbuf[slot].reshape(n_cols, d).astype(_BF16)
        z = lax.dot_general(q, kp, _NT, preferred_element_type=_F32)
        log_1m = jnp.where(own, _log_one_minus_beta(z), 0.0)
        running = carry
        w_tiles = []
        for c in reversed(range(n_cols // V7X_LANES)):
            sl = slice(c * V7X_LANES, (c + 1) * V7X_LANES)
            incl = _suffix_sums(log_1m[:, sl], tmat)
            w_tiles.append(jnp.where(own[:, sl], jnp.exp(z[:, sl] + incl + running), 0.0))
            running = running + jnp.sum(log_1m[:, sl], axis=1, keepdims=True)
        w = jnp.concatenate(w_tiles[::-1], axis=1)
        acc_ref[...] += jnp.dot(w.astype(_BF16), vp, preferred_element_type=_F32)
        return j + 1, running, jnp.max(running) > EXP_UNDERFLOW

    j_end, _, _ = lax.while_loop(cond, body, (jnp.int32(0), jnp.zeros((nh, 1), _F32), jnp.bool_(True)))

    @pl.when(j_end < n_pages)
    def _():
        wait(j_end, lax.rem(j_end, 2))

    o_ref[...] = acc_ref[...]


def _page_specs(page_shape, layer, pp):
    def spec(slot):
        return pl.BlockSpec((None, None) + tuple(page_shape),
                            lambda b, p, pt: (layer, pt[b, p * pp + slot], 0, 0, 0))
    return [spec(slot) for slot in range(pp)]


def _sample_diff_attention(page_table, lam_p, g, q, k_new, v_new, cache_k, cache_v, layer, *, lam_init):
    n, nh, hw = q.shape
    n_pages = page_table.shape[1]
    page_shape = cache_k.shape[2:]
    pp = _tile(n_pages, PAGES_PER_STEP)
    head_spec = pl.BlockSpec((None, nh, hw), lambda b, p, pt: (b, 0, 0))
    pages = _page_specs(page_shape, layer, pp)
    page_elems = math.prod(page_shape)
    est = 2 * 2 * pp * page_elems * 4 + 4 * page_elems * 2 + 4 * 2 * nh * page_shape[0] * nh * 4
    return pl.pallas_call(
        functools.partial(_sample_diff_kernel, pp=pp, lam_init=lam_init),
        grid_spec=pltpu.PrefetchScalarGridSpec(
            num_scalar_prefetch=1,
            grid=(n, n_pages // pp),
            in_specs=[
                pl.BlockSpec((4, HEAD_DIM), lambda b, p, pt: (0, 0)),
                pl.BlockSpec((1, hw), lambda b, p, pt: (0, 0)),
                head_spec, head_spec, head_spec,
            ] + pages + pages,
            out_specs=head_spec,
            scratch_shapes=[pltpu.VMEM((2 * nh, hw), _F32), pltpu.VMEM((2 * nh, 1), _F32),
                            pltpu.VMEM((2 * nh, 1), _F32), pltpu.VMEM((2 * nh, hw), _F32)],
        ),
        out_shape=jax.ShapeDtypeStruct((n, nh, hw), _F32),
        compiler_params=_params(("parallel", "arbitrary"), est),
        name="sample_diff_attn",
    )(page_table, lam_p, g, q, k_new, v_new, *([cache_k] * pp), *([cache_v] * pp))


def _sample_sb_attention(page_table, q, cache_k, cache_v, layer):
    n, nh, d = q.shape
    n_pages = page_table.shape[1]
    page_shape = tuple(cache_k.shape[2:])
    head_spec = pl.BlockSpec((None, nh, d), lambda b, pt: (b, 0, 0))
    page_elems = math.prod(page_shape)
    est = 2 * 2 * page_elems * 4 + 4 * page_elems * 2 + 8 * nh * page_shape[0] * nh * 4
    return pl.pallas_call(
        functools.partial(_sample_sb_kernel, layer=layer, n_pages=n_pages),
        grid_spec=pltpu.PrefetchScalarGridSpec(
            num_scalar_prefetch=1,
            grid=(n,),
            in_specs=[head_spec, pl.BlockSpec(memory_space=pl.ANY), pl.BlockSpec(memory_space=pl.ANY)],
            out_specs=head_spec,
            scratch_shapes=[pltpu.VMEM((2,) + page_shape, _F32), pltpu.VMEM((2,) + page_shape, _F32),
                            pltpu.SemaphoreType.DMA((2, 2)), pltpu.VMEM((nh, d), _F32)],
        ),
        out_shape=jax.ShapeDtypeStruct((n, nh, d), _F32),
        compiler_params=_params(("arbitrary",), est),
        name="sample_sb_attn",
    )(page_table, q, cache_k, cache_v)


def _layer_norm_rows(z, g, b):
    mu = jnp.mean(z, axis=1, keepdims=True)
    zc = z - mu
    var = jnp.mean(zc * zc, axis=1, keepdims=True)
    return zc * lax.rsqrt(var + LN_EPS) * g + b


def _out_proj_kernel(od_ref, os_ref, x_ref, w_ref, g_ref, b_ref, hf_ref, hb_ref, *, alpha, kd_tiles):
    k = pl.program_id(1)

    @pl.when(k == 0)
    def _():
        hf_ref[...] = alpha * x_ref[...]

    lhs = jnp.where(k < kd_tiles, od_ref[...], os_ref[...])
    hf_ref[...] += jnp.dot(lhs, w_ref[...], preferred_element_type=_F32)

    @pl.when(k == pl.num_programs(1) - 1)
    def _():
        h = _layer_norm_rows(hf_ref[...], g_ref[...], b_ref[...])
        hf_ref[...] = h
        hb_ref[...] = h.astype(hb_ref.dtype)


def _out_proj(o_d, o_s, x, w_b, g, b, *, alpha, tm_pref):
    m, d = x.shape
    wd, ws = o_d.shape[1], o_s.shape[1]
    tm = _tile(m, tm_pref)
    tk = _tile(math.gcd(wd, ws), 512)
    kd_tiles, ks_tiles = wd // tk, ws // tk
    est = 2 * (2 * tm * tk * 2 + tk * d * 2 + tm * d * 6) + 2 * tm * d * 4
    return pl.pallas_call(
        functools.partial(_out_proj_kernel, alpha=alpha, kd_tiles=kd_tiles),
        grid=(m // tm, kd_tiles + ks_tiles),
        in_specs=[
            pl.BlockSpec((tm, tk), lambda i, k: (i, jnp.minimum(k, kd_tiles - 1))),
            pl.BlockSpec((tm, tk), lambda i, k: (i, jnp.maximum(k - kd_tiles, 0))),
            pl.BlockSpec((tm, d), lambda i, k: (i, 0), pipeline_mode=pl.Buffered(1)),
            pl.BlockSpec((tk, d), lambda i, k: (k, 0)),
            pl.BlockSpec((1, d), lambda i, k: (0, 0)),
            pl.BlockSpec((1, d), lambda i, k: (0, 0)),
        ],
        out_specs=[pl.BlockSpec((tm, d), lambda i, k: (i, 0)), pl.BlockSpec((tm, d), lambda i, k: (i, 0))],
        out_shape=[jax.ShapeDtypeStruct((m, d), _F32), jax.ShapeDtypeStruct((m, d), _BF16)],
        compiler_params=_params(("parallel", "arbitrary"), est),
        name="out_proj_ln",
    )(o_d, o_s, x, w_b, g, b)


def _mlp_kernel(hb_ref, hf_ref, wu_ref, wd_ref, g_ref, b_ref, y_ref, *, alpha):
    j = pl.program_id(1)

    @pl.when(j == 0)
    def _():
        y_ref[...] = alpha * hf_ref[...]

    a = jnp.maximum(jnp.dot(hb_ref[...], wu_ref[...], preferred_element_type=_F32), 0.0)
    y_ref[...] += jnp.dot((a * a).astype(_BF16), wd_ref[...], preferred_element_type=_F32)

    @pl.when(j == pl.num_programs(1) - 1)
    def _():
        y_ref[...] = _layer_norm_rows(y_ref[...], g_ref[...], b_ref[...])


def _mlp(h_b, h_f, wu_b, wd_b, g, b, *, alpha, tm_pref):
    m, d = h_f.shape
    dff = wu_b.shape[1]
    tm = _tile(m, tm_pref)
    tf = _tile(dff, 512)
    est = 2 * tm * d * 2 + tm * d * 4 + 2 * 2 * d * tf * 2 + tm * d * 4 + tm * tf * 6 + tm * d * 4
    single = pl.Buffered(1)
    return pl.pallas_call(
        functools.partial(_mlp_kernel, alpha=alpha),
        grid=(m // tm, dff // tf),
        in_specs=[
            pl.BlockSpec((tm, d), lambda i, j: (i, 0)),
            pl.BlockSpec((tm, d), lambda i, j: (i, 0), pipeline_mode=single),
            pl.BlockSpec((d, tf), lambda i, j: (0, j)),
            pl.BlockSpec((tf, d), lambda i, j: (j, 0)),
            pl.BlockSpec((1, d), lambda i, j: (0, 0)),
            pl.BlockSpec((1, d), lambda i, j: (0, 0)),
        ],
        out_specs=pl.BlockSpec((tm, d), lambda i, j: (i, 0), pipeline_mode=single),
        out_shape=jax.ShapeDtypeStruct((m, d), _F32),
        compiler_params=_params(("parallel", "arbitrary"), est),
        name="mlp_ln",
    )(h_b, h_f, wu_b, wd_b, g, b)


def _rope_tables(pos):
    inv = ROPE_THETA ** (-jnp.arange(0, HEAD_DIM, 2, dtype=_F32) / HEAD_DIM)
    ang = pos.astype(_F32)[:, None] * inv[None, :]
    cos, sin = jnp.cos(ang), jnp.sin(ang)
    return jnp.concatenate([cos, cos], axis=1), jnp.concatenate([-sin, sin], axis=1)


def kernel(x_prompt, x_sample, cache_diff_k, cache_diff_v, cache_sb_k, cache_sb_v, page_table, w_in, lambda_q1, lambda_k1, lambda_q2, lambda_k2, subln_g, w_out, ln1_g, ln1_b, w_up, w_down, ln2_g, ln2_b):
    bsz, seq, d = x_prompt.shape
    n_dec, dec_seq, _ = x_sample.shape
    assert dec_seq == 1
    depth, n_pool, page, h_diff, hw = cache_diff_k.shape
    h_sb = cache_sb_k.shape[3]
    assert hw == 2 * HEAD_DIM and cache_sb_k.shape[4] == HEAD_DIM
    w_diff, w_sb = h_diff * hw, h_sb * HEAD_DIM
    past_len = page_table.shape[1] * page
    alpha = (2.0 * depth) ** 0.25
    scale = HEAD_DIM ** -0.5

    cos_p, sin_p = _rope_tables(jnp.arange(seq, dtype=jnp.int32))
    cos_s, sin_s = _rope_tables(jnp.full((n_dec,), past_len, dtype=jnp.int32))

    sections = {
        "dq": (0, w_diff, True, scale * LOG2_E), "dk": (w_diff, w_diff, True, 1.0), "dv": (2 * w_diff, w_diff, False, 1.0),
        "sq": (3 * w_diff, w_sb, False, scale), "sk": (3 * w_diff + w_sb, w_sb, False, 1.0),
        "sv": (3 * w_diff + 2 * w_sb, w_sb, False, 1.0),
    }

    xp = x_prompt.reshape(bsz * seq, d)
    xs = x_sample.reshape(n_dec, d)
    outs = {k: [] for k in ("pdk", "pdv", "psk", "psv", "sdk", "sdv", "ssk", "ssv")}
    for l in range(depth):
        lam_init = _lambda_init(l)
        lam_p = jnp.stack([lambda_q1[l], lambda_k1[l], lambda_q2[l], lambda_k2[l]]).astype(_F32)
        g_sub = subln_g[l].reshape(1, hw).astype(_F32)
        w_in_b, w_out_b = w_in[l].astype(_BF16), w_out[l].astype(_BF16)
        w_up_b, w_down_b = w_up[l].astype(_BF16), w_down[l].astype(_BF16)
        ln1 = (ln1_g[l].reshape(1, d), ln1_b[l].reshape(1, d))
        ln2 = (ln2_g[l].reshape(1, d), ln2_b[l].reshape(1, d))

        xp_b = xp.astype(_BF16)
        pr = {}
        for name, (off, width, rope, sc) in sections.items():
            dts = (_BF16,) if name in ("dq", "sq") else (_F32, _BF16)
            pr[name] = _project(xp_b, w_in_b, cos_p, sin_p, off, width, rope=rope, scale=sc,
                                out_dtypes=dts, tm_pref=1024, tn_pref=1024)
        as3 = lambda a: a.reshape(bsz, seq, a.shape[1])
        o_d = _diff_attention(lam_p, g_sub, as3(pr["dq"][0]), as3(pr["dk"][1]), as3(pr["dv"][1]), lam_init=lam_init)
        o_s = _sb_attention(as3(pr["sq"][0]), as3(pr["sk"][1]), as3(pr["sv"][1]))
        outs["pdk"].append(pr["dk"][0].reshape(bsz, seq, h_diff, hw))
        outs["pdv"].append(pr["dv"][0].reshape(bsz, seq, h_diff, hw))
        outs["psk"].append(pr["sk"][0].reshape(bsz, seq, h_sb, HEAD_DIM))
        outs["psv"].append(pr["sv"][0].reshape(bsz, seq, h_sb, HEAD_DIM))
        h_f, h_b = _out_proj(o_d.reshape(bsz * seq, w_diff), o_s.reshape(bsz * seq, w_sb), xp, w_out_b,
                             *ln1, alpha=alpha, tm_pref=512)
        xp_next = _mlp(h_b, h_f, w_up_b, w_down_b, *ln2, alpha=alpha, tm_pref=512)

        xs_b = xs.astype(_BF16)
        assert all(rope == (off < 2 * w_diff) for off, _, rope, _ in sections.values())
        u_s = _project_rows(xs_b, w_in_b, cos_s, sin_s, 2 * w_diff,
                            [(off, off + width, sc) for off, width, _, sc in sections.values() if sc != 1.0])
        sm = {name: u_s[:, off:off + width] for name, (off, width, _, _) in sections.items()}
        heads_d = lambda a: a.reshape(n_dec, h_diff, hw)
        od_s = _sample_diff_attention(page_table, lam_p, g_sub, heads_d(sm["dq"]), heads_d(sm["dk"]),
                                      heads_d(sm["dv"]), cache_diff_k, cache_diff_v, l, lam_init=lam_init)
        os_s = _sample_sb_attention(page_table, sm["sq"].reshape(n_dec, h_sb, HEAD_DIM), cache_sb_k, cache_sb_v, l)
        outs["sdk"].append(sm["dk"].reshape(n_dec, dec_seq, h_diff, hw))
        outs["sdv"].append(sm["dv"].reshape(n_dec, dec_seq, h_diff, hw))
        outs["ssk"].append(sm["sk"].reshape(n_dec, dec_seq, h_sb, HEAD_DIM))
        outs["ssv"].append(sm["sv"].reshape(n_dec, dec_seq, h_sb, HEAD_DIM))
        hs_f, hs_b = _out_proj(od_s.reshape(n_dec, w_diff).astype(_BF16), os_s.reshape(n_dec, w_sb).astype(_BF16),
                               xs, w_out_b, *ln1, alpha=alpha, tm_pref=n_dec)
        xs = _mlp(hs_b, hs_f, w_up_b, w_down_b, *ln2, alpha=alpha, tm_pref=n_dec)
        xp = xp_next

    stack = lambda k: jnp.stack(outs[k])
    return (xp.reshape(bsz, seq, d), xs.reshape(n_dec, dec_seq, d),
            stack("pdk"), stack("pdv"), stack("psk"), stack("psv"),
            stack("sdk"), stack("sdv"), stack("ssk"), stack("ssv"))
```
